```python
import jax, jax.numpy as jnp
from jax import lax
import numpy as np


D_MODEL = 2048
BATCH = 8
SEQ = 2048
DEPTH = 2

A_HEADS = 16
A_HEAD_DIM = 128
A_KV_RANK = 256
IDX_HEADS = 16
IDX_DIM = 64
TOPK_MAX = 256
Q_BLOCK = 128
M_HEADS = 8
M_QK_DIM = 128
M_V_DIM = 256
M_CHUNK = 64
N_GROUPS = 4
EXPERTS_PER_GROUP = 4
N_EXPERTS = N_GROUPS * EXPERTS_PER_GROUP
TOP_K_IN_GROUP = 2
EXPERT_FF = 512
PLE_DIM = 256
LN_EPS = 1e-5
DN_ALPHA = (2 * DEPTH) ** 0.25
DN_BETA = (8 * DEPTH) ** -0.25

_SPLITS = (
    ('a_q', A_HEADS * A_HEAD_DIM),
    ('a_ckv', A_KV_RANK),
    ('i_q', IDX_HEADS * IDX_DIM),
    ('i_k', IDX_DIM),
    ('i_w', IDX_HEADS),
    ('m_q', M_HEADS * M_QK_DIM),
    ('m_k', M_HEADS * M_QK_DIM),
    ('m_v', M_HEADS * M_V_DIM),
    ('m_o', M_HEADS * M_V_DIM),
    ('m_i', M_HEADS),
    ('m_f', M_HEADS),
    ('g_a', D_MODEL),
    ('g_m', D_MODEL),
)
IN_WIDTH = sum(n for _, n in _SPLITS)

kernel_name = 'hybrid_dsa_mlstm_hmoe_deepnorm'

F32 = jnp.float32


def _split_cols(a):
    offs = np.cumsum([n for _, n in _SPLITS])[:-1].tolist()
    return jnp.split(a, offs, axis=-1)


def layer_norm(x, g, b):
    xf = x.astype(F32)
    mu = jnp.mean(xf, -1, keepdims=True)
    var = jnp.mean(jnp.square(xf - mu), -1, keepdims=True)
    return ((xf - mu) * lax.rsqrt(var + LN_EPS) * g.astype(F32) + b.astype(F32)).astype(x.dtype)


def rms_norm(x, g):
    xf = x.astype(F32)
    return (xf * lax.rsqrt(jnp.mean(jnp.square(xf), -1, keepdims=True) + LN_EPS) * g.astype(F32)).astype(x.dtype)


def alibi_slopes(n):
    return jnp.exp2(-8.0 * jnp.arange(1, n + 1, dtype=F32) / n)


def dsa_attention(q, kv, iq, ik, iw):
    B, S = q.shape[0], q.shape[1]
    n_keep = min(TOPK_MAX, S // 4)
    n_blocks = S // Q_BLOCK
    slopes = alibi_slopes(A_HEADS)
    key_pos = jnp.arange(S)
    ikf = ik.astype(F32)

    def block(i):
        t0 = i * Q_BLOCK
        qb = lax.dynamic_slice_in_dim(q, t0, Q_BLOCK, axis=1).astype(F32)
        iqb = lax.dynamic_slice_in_dim(iq, t0, Q_BLOCK, axis=1).astype(F32)
        iwb = lax.dynamic_slice_in_dim(iw, t0, Q_BLOCK, axis=1).astype(F32)
        qpos = t0 + jnp.arange(Q_BLOCK)
        rel = jax.nn.relu(jnp.einsum('bqhd,bsd->bqhs', iqb, ikf) * IDX_DIM ** -0.5)
        score = jnp.einsum('bqh,bqhs->bqs', iwb * IDX_HEADS ** -0.5, rel)
        causal = key_pos[None, :] <= qpos[:, None]
        score = jnp.where(causal[None], score, -jnp.inf)
        _, sel = lax.top_k(score, n_keep)
        kv_sel = jax.vmap(lambda kvb, ib: kvb[ib])(kv, sel).astype(F32)
        k_sel, v_sel = kv_sel[..., :A_HEAD_DIM], kv_sel[..., A_HEAD_DIM:]
        logits = jnp.einsum('bqhd,bqnd->bqhn', qb, k_sel) * A_HEAD_DIM ** -0.5
        dist = (qpos[None, :, None] - sel).astype(F32)
        logits = logits - slopes[None, None, :, None] * dist[:, :, None, :]
        valid = sel <= qpos[None, :, None]
        logits = jnp.where(valid[:, :, None, :], logits, -jnp.inf)
        probs = jax.nn.softmax(logits, axis=-1)
        return jnp.einsum('bqhn,bqnd->bqhd', probs, v_sel).astype(q.dtype)

    out = lax.map(block, jnp.arange(n_blocks))
    return out.transpose(1, 0, 2, 3, 4).reshape(B, S, A_HEADS * A_HEAD_DIM)


def mlstm(q, k, v, i_pre, f_pre):
    B, S, H, dk = q.shape
    dv = v.shape[-1]
    L = M_CHUNK
    nc = S // L

    def chunks(a):
        return a.astype(F32).reshape(B, nc, L, H, -1).transpose(0, 3, 1, 2, 4)

    def gchunks(a):
        return a.astype(F32).reshape(B, nc, L, H).transpose(0, 3, 1, 2)

    qc = chunks(q) * dk ** -0.5
    kc = chunks(k)
    vc = chunks(v)
    li = gchunks(i_pre)
    lf = jax.nn.log_sigmoid(gchunks(f_pre))
    bcum = jnp.cumsum(lf, axis=-1)
    b_last = bcum[..., -1]
    tri = jnp.tril(jnp.ones((L, L), dtype=bool))
    dmat = jnp.where(tri, bcum[..., :, None] - bcum[..., None, :] + li[..., None, :], -jnp.inf)
    w_end = b_last[..., None] - bcum + li
    a_end = jnp.max(w_end, axis=-1)
    kw = kc * jnp.exp(w_end - a_end[..., None])[..., None]

    def step(carry, xs):
        C, n, m = carry
        q_c, kw_c, v_c, a_c, bl_c = xs
        inter_num = jnp.einsum('bhlk,bhkv->bhlv', q_c, C)
        inter_den = jnp.einsum('bhlk,bhk->bhl', q_c, n)
        m_new = jnp.maximum(bl_c + m, a_c)
        decay = jnp.exp(bl_c + m - m_new)
        scale = jnp.exp(a_c - m_new)
        C_new = decay[..., None, None] * C + scale[..., None, None] * jnp.einsum('bhlk,bhlv->bhkv', kw_c, v_c)
        n_new = decay[..., None] * n + scale[..., None] * jnp.sum(kw_c, axis=-2)
        return (C_new, n_new, m_new), (inter_num, inter_den, m)

    init = (jnp.zeros((B, H, dk, dv), F32), jnp.zeros((B, H, dk), F32), jnp.zeros((B, H), F32))
    xs = (jnp.moveaxis(qc, 2, 0), jnp.moveaxis(kw, 2, 0), jnp.moveaxis(vc, 2, 0),
          jnp.moveaxis(a_end, 2, 0), jnp.moveaxis(b_last, 2, 0))
    _, (inter_num, inter_den, m_prev) = lax.scan(step, init, xs)
    inter_num = jnp.moveaxis(inter_num, 0, 2)
    inter_den = jnp.moveaxis(inter_den, 0, 2)
    m_prev = jnp.moveaxis(m_prev, 0, -1)

    inter_log = bcum + m_prev[..., None]
    m_row = jnp.maximum(inter_log, jnp.max(dmat, axis=-1))
    inter_w = jnp.exp(inter_log - m_row)
    intra_w = jnp.exp(dmat - m_row[..., None]) * jnp.einsum('bhcjk,bhcsk->bhcjs', qc, kc)
    num = inter_w[..., None] * inter_num + jnp.einsum('bhcjs,bhcsv->bhcjv', intra_w, vc)
    den = inter_w * inter_den + jnp.sum(intra_w, axis=-1)
    h = num / jnp.maximum(jnp.abs(den), jnp.exp(-m_row))[..., None]
    return h.transpose(0, 2, 3, 1, 4).reshape(B, S, H * dv)


def hier_moe(x, w_grp, b_grp, w_exp, b_exp, w_gate, w_up, w_down):
    B, S, D = x.shape
    xt = x.reshape(-1, D)
    xf = xt.astype(F32)
    grp_prob = jax.nn.softmax(xf @ w_grp.astype(F32) + b_grp.astype(F32), axis=-1)
    g_p, g_idx = lax.top_k(grp_prob, 1)
    exp_logits = (xf @ w_exp.astype(F32) + b_exp.astype(F32)).reshape(-1, N_GROUPS, EXPERTS_PER_GROUP)
    in_grp = jnp.take_along_axis(exp_logits, g_idx[:, :, None], axis=1)[:, 0]
    e_prob = jax.nn.softmax(in_grp, axis=-1)
    e_p, e_idx = lax.top_k(e_prob, TOP_K_IN_GROUP)
    e_p = e_p / jnp.sum(e_p, axis=-1, keepdims=True)
    w_in_grp = jnp.einsum('tk,tke->te', e_p, jax.nn.one_hot(e_idx, EXPERTS_PER_GROUP, dtype=F32))
    combine = (g_p[:, :, None] * jax.nn.one_hot(g_idx[:, 0], N_GROUPS, dtype=F32)[:, :, None]
               * w_in_grp[:, None, :]).reshape(-1, N_EXPERTS)
    hg = jnp.einsum('td,edf->tef', xt, w_gate)
    hu = jnp.einsum('td,edf->tef', xt, w_up)
    h = jax.nn.silu(hg) * hu * combine[:, :, None].astype(hg.dtype)
    return jnp.einsum('tef,efd->td', h, w_down).reshape(B, S, D)


def setup_inputs(seed: int = 0) -> dict:
    key = jax.random.key(seed)
    ks = jax.random.split(key, 24)

    def w(k, shape, fan_in, scale=1.0):
        return jax.random.normal(k, shape, F32) * (scale * fan_in ** -0.5)

    def noise(k, shape, s):
        return jax.random.normal(k, shape, F32) * s

    return {
        'x': jax.random.normal(ks[0], (BATCH, SEQ, D_MODEL), F32),
        'p': jax.random.normal(ks[1], (DEPTH, BATCH, SEQ, PLE_DIM), F32),
        'w_in': w(ks[2], (DEPTH, D_MODEL, IN_WIDTH), D_MODEL),
        'kv_norm': 1.0 + noise(ks[3], (DEPTH, A_KV_RANK), 0.02),
        'w_ukv': w(ks[4], (DEPTH, A_KV_RANK, 2 * A_HEAD_DIM), A_KV_RANK),
        'm_i_bias': noise(ks[5], (DEPTH, M_HEADS), 0.1),
        'm_f_bias': 3.0 + noise(ks[6], (DEPTH, M_HEADS), 0.5),
        'w_proj_a': w(ks[7], (DEPTH, A_HEADS * A_HEAD_DIM, D_MODEL), A_HEADS * A_HEAD_DIM),
        'w_proj_m': w(ks[8], (DEPTH, M_HEADS * M_V_DIM, D_MODEL), M_HEADS * M_V_DIM),
        'w_out': w(ks[9], (DEPTH, D_MODEL, D_MODEL), D_MODEL, DN_BETA),
        'ln1_g': 1.0 + noise(ks[10], (DEPTH, D_MODEL), 0.02),
        'ln1_b': noise(ks[11], (DEPTH, D_MODEL), 0.02),
        'w_grp': w(ks[12], (DEPTH, D_MODEL, N_GROUPS), D_MODEL),
        'b_grp': noise(ks[13], (DEPTH, N_GROUPS), 0.01),
        'w_exp': w(ks[14], (DEPTH, D_MODEL, N_EXPERTS), D_MODEL),
        'b_exp': noise(ks[15], (DEPTH, N_EXPERTS), 0.01),
        'w_gate': w(ks[16], (DEPTH, N_EXPERTS, D_MODEL, EXPERT_FF), D_MODEL),
        'w_up': w(ks[17], (DEPTH, N_EXPERTS, D_MODEL, EXPERT_FF), D_MODEL),
        'w_down': w(ks[18], (DEPTH, N_EXPERTS, EXPERT_FF, D_MODEL), EXPERT_FF, DN_BETA),
        'w_ple': w(ks[19], (DEPTH, PLE_DIM, D_MODEL), PLE_DIM, DN_BETA),
        'w_ple_gate': w(ks[20], (DEPTH, D_MODEL, D_MODEL), D_MODEL),
        'ln2_g': 1.0 + noise(ks[21], (DEPTH, D_MODEL), 0.02),
        'ln2_b': noise(ks[22], (DEPTH, D_MODEL), 0.02),
    }


def reference(x, p, w_in, kv_norm, w_ukv, m_i_bias, m_f_bias, w_proj_a, w_proj_m, w_out, ln1_g, ln1_b,
              w_grp, b_grp, w_exp, b_exp, w_gate, w_up, w_down, w_ple, w_ple_gate, ln2_g, ln2_b):
    B, S, _ = x.shape
    h = x
    for l in range(DEPTH):
        (a_q, a_ckv, i_q, i_k, i_w, m_q, m_k, m_v, m_o, m_i, m_f, g_a, g_m) = _split_cols(h @ w_in[l])
        kv = rms_norm(a_ckv, kv_norm[l]) @ w_ukv[l]
        y_a = dsa_attention(a_q.reshape(B, S, A_HEADS, A_HEAD_DIM), kv,
                            i_q.reshape(B, S, IDX_HEADS, IDX_DIM), i_k, i_w)
        y_m = mlstm(m_q.reshape(B, S, M_HEADS, M_QK_DIM), m_k.reshape(B, S, M_HEADS, M_QK_DIM),
                    m_v.reshape(B, S, M_HEADS, M_V_DIM), m_i + m_i_bias[l], m_f + m_f_bias[l])
        y_m = (y_m * jax.nn.sigmoid(m_o.astype(F32))).astype(h.dtype)
        mix = jax.nn.sigmoid(g_a) * (y_a @ w_proj_a[l]) + jax.nn.sigmoid(g_m) * (y_m @ w_proj_m[l])
        h = layer_norm(DN_ALPHA * h + mix @ w_out[l], ln1_g[l], ln1_b[l])
        ff = hier_moe(h, w_grp[l], b_grp[l], w_exp[l], b_exp[l], w_gate[l], w_up[l], w_down[l])
        ple = (p[l] @ w_ple[l]) * jax.nn.sigmoid(h @ w_ple_gate[l])
        h = layer_norm(DN_ALPHA * h + ff + ple, ln2_g[l], ln2_b[l])
    return h
```

```python
import functools

import jax
import jax.numpy as jnp
from jax import lax
from jax.experimental import pallas as pl
from jax.experimental.pallas import tpu as pltpu

F32 = jnp.float32
BF16 = jnp.bfloat16
I32 = jnp.int32

D_MODEL = 2048
DEPTH = 2
A_HEADS = 16
A_HEAD_DIM = 128
A_KV_RANK = 256
IDX_HEADS = 16
IDX_DIM = 64
TOPK_MAX = 256
M_HEADS = 8
M_QK_DIM = 128
M_V_DIM = 256
N_GROUPS = 4
EXPERTS_PER_GROUP = 4
N_EXPERTS = N_GROUPS * EXPERTS_PER_GROUP
EXPERT_FF = 512
PLE_DIM = 256
LN_EPS = 1e-5
DN_ALPHA = (2 * DEPTH) ** 0.25

_SPLITS = (
    ("a_q", A_HEADS * A_HEAD_DIM), ("a_ckv", A_KV_RANK), ("i_q", IDX_HEADS * IDX_DIM), ("i_k", IDX_DIM),
    ("i_w", IDX_HEADS), ("m_q", M_HEADS * M_QK_DIM), ("m_k", M_HEADS * M_QK_DIM), ("m_v", M_HEADS * M_V_DIM),
    ("m_o", M_HEADS * M_V_DIM), ("m_i", M_HEADS), ("m_f", M_HEADS), ("g_a", D_MODEL), ("g_m", D_MODEL),
)

C_AQ, C_MQ, C_MK, C_MV, C_MO, C_GA, C_GM = 0, 2048, 3072, 4096, 6144, 8192, 10240
C_IQ, C_CKV, C_IKA, C_IKB = 12288, 13312, 13568, 13696
N_MAIN = 13824
G_IW, G_MI, G_MF = 0, 16, 24
LANES = 128
ROUTE_W = LANES
HX_W = D_MODEL + ROUTE_W

INT_MIN = -(2 ** 31)
NEG_BIG = -1e30
VMEM_LIMIT = 56 * 1024 * 1024

_NT = (((1,), (1,)), ((), ()))
_TN = (((0,), (0,)), ((), ()))


def _cparams(sem):
    return pltpu.CompilerParams(dimension_semantics=sem, vmem_limit_bytes=VMEM_LIMIT)


def _inproj_body(x_ref, w_ref, ws_ref, wst_ref, brow_ref, bcol_ref, o_ref, g_ref, gt_ref, xb_ref):
    @pl.when(pl.program_id(1) == 0)
    def _():
        xb = x_ref[...].astype(BF16)
        xb_ref[...] = xb
        g_ref[...] = jnp.dot(xb, ws_ref[...], preferred_element_type=F32) + brow_ref[...]
        gt_ref[...] = lax.dot_general(wst_ref[...], xb, _NT, preferred_element_type=F32) + bcol_ref[...]

    o_ref[...] = jnp.dot(xb_ref[...], w_ref[...], preferred_element_type=F32).astype(BF16)


def _inproj(x, wp, ws, wst, brow, bcol, tm, tn):
    T = x.shape[0]
    return pl.pallas_call(
        _inproj_body,
        grid=(T // tm, N_MAIN // tn),
        in_specs=[
            pl.BlockSpec((tm, D_MODEL), lambda i, j: (i, 0)),
            pl.BlockSpec((D_MODEL, tn), lambda i, j: (0, j)),
            pl.BlockSpec((D_MODEL, LANES), lambda i, j: (0, 0)),
            pl.BlockSpec((LANES, D_MODEL), lambda i, j: (0, 0)),
            pl.BlockSpec((1, LANES), lambda i, j: (0, 0)),
            pl.BlockSpec((LANES, 1), lambda i, j: (0, 0)),
        ],
        out_specs=[
            pl.BlockSpec((tm, tn), lambda i, j: (i, j)),
            pl.BlockSpec((tm, LANES), lambda i, j: (i, 0)),
            pl.BlockSpec((LANES, tm), lambda i, j: (0, i)),
        ],
        out_shape=[
            jax.ShapeDtypeStruct((T, N_MAIN), BF16),
            jax.ShapeDtypeStruct((T, LANES), F32),
            jax.ShapeDtypeStruct((LANES, T), F32),
        ],
        scratch_shapes=[pltpu.VMEM((tm, D_MODEL), BF16)],
        compiler_params=_cparams(("arbitrary", "arbitrary")),
        name="inproj",
    )(x, wp, ws, wst, brow, bcol)


def _kv_body(c_ref, gn_ref, w_ref, o_ref):
    c = c_ref[...].astype(F32)
    r = c * lax.rsqrt(jnp.mean(c * c, axis=-1, keepdims=True) + LN_EPS) * gn_ref[...]
    o_ref[...] = jnp.dot(r.astype(BF16), w_ref[...], preferred_element_type=F32).astype(BF16)


def _kv(proj, gn, wukv, tm):
    T = proj.shape[0]
    return pl.pallas_call(
        _kv_body,
        grid=(T // tm,),
        in_specs=[
            pl.BlockSpec((tm, A_KV_RANK), lambda i: (i, C_CKV // A_KV_RANK)),
            pl.BlockSpec((1, A_KV_RANK), lambda i: (0, 0)),
            pl.BlockSpec((A_KV_RANK, 2 * A_HEAD_DIM), lambda i: (0, 0)),
        ],
        out_specs=pl.BlockSpec((tm, 2 * A_HEAD_DIM), lambda i: (i, 0)),
        out_shape=jax.ShapeDtypeStruct((T, 2 * A_HEAD_DIM), BF16),
        compiler_params=_cparams(("arbitrary",)),
        name="kv",
    )(proj, gn, wukv)


def _attn_body(q_ref, iq_ref, ika_ref, ikb_ref, kv_ref, g_ref, o_ref, key_ref, *, tq, n_keep):
    qi = pl.program_id(1)
    nck = qi + 1
    t0 = qi * tq
    row_t = t0 + lax.broadcasted_iota(I32, (tq, tq), 0)
    col_l = lax.broadcasted_iota(I32, (tq, tq), 1)

    wq = g_ref[:, G_IW:G_IW + IDX_HEADS] * (IDX_HEADS ** -0.5 * IDX_DIM ** -0.5)

    def score_chunk(c, carry):
        k0 = pl.multiple_of(c * tq, tq)
        ka = ika_ref[pl.ds(k0, tq), :]
        kb = ikb_ref[pl.ds(k0, tq), :]
        acc = jnp.zeros((tq, tq), F32)
        for p in range(IDX_HEADS // 2):
            pair = iq_ref[:, p * LANES:(p + 1) * LANES]
            xe = lax.dot_general(pair, ka, _NT, preferred_element_type=F32)
            xo = lax.dot_general(pair, kb, _NT, preferred_element_type=F32)
            acc = acc + wq[:, 2 * p:2 * p + 1] * jnp.maximum(xe, 0.0)
            acc = acc + wq[:, 2 * p + 1:2 * p + 2] * jnp.maximum(xo, 0.0)
        bits = lax.bitcast_convert_type(acc, I32)
        key = bits ^ ((bits >> 31) & 0x7FFFFFFF)
        key_ref[c] = jnp.where(k0 + col_l <= row_t, key, INT_MIN)
        return carry

    lax.fori_loop(0, nck, score_chunk, 0)

    def bisect(b, ans_u):
        cand_u = ans_u | lax.shift_left(jnp.int32(1), 31 - b)
        cand_s = cand_u ^ INT_MIN

        def count_chunk(c, cnt):
            hit = jnp.where(key_ref[c] >= cand_s, 1.0, 0.0)
            for v in range(tq // LANES):
                cnt = cnt + hit[:, v * LANES:(v + 1) * LANES]
            return cnt

        cnt = lax.fori_loop(0, nck, count_chunk, jnp.zeros((tq, LANES), F32))
        tot = jnp.sum(cnt, axis=-1, keepdims=True)
        return jnp.where(tot >= n_keep, cand_u, ans_u)

    ans_u = lax.fori_loop(0, 32, bisect, jnp.zeros((tq, 1), I32))
    tau = jnp.where(row_t[:, :1] < n_keep, INT_MIN + 1, ans_u ^ INT_MIN)

    scale = A_HEAD_DIM ** -0.5
    for h in range(A_HEADS):
        qh = q_ref[:, h * A_HEAD_DIM:(h + 1) * A_HEAD_DIM]
        slope = 2.0 ** (-8.0 * (h + 1) / A_HEADS)

        def att_chunk(c, carry, qh=qh, slope=slope):
            m, l, acc = carry
            k0 = pl.multiple_of(c * tq, tq)
            kc = kv_ref[pl.ds(k0, tq), 0:A_HEAD_DIM]
            vc = kv_ref[pl.ds(k0, tq), A_HEAD_DIM:2 * A_HEAD_DIM]
            s = lax.dot_general(qh, kc, _NT, preferred_element_type=F32)
            rel = (k0 - t0 + lax.broadcasted_iota(I32, (1, tq), 1)).astype(F32)
            msk = key_ref[c] >= tau
            lg = jnp.where(msk, s * scale + slope * rel, NEG_BIG)
            m_new = jnp.maximum(m, jnp.max(lg, axis=-1, keepdims=True))
            alpha = jnp.exp(m - m_new)
            p = jnp.where(msk, jnp.exp(lg - m_new), 0.0)
            l = alpha * l + jnp.sum(p, axis=-1, keepdims=True)
            acc = alpha * acc + jnp.dot(p.astype(BF16), vc, preferred_element_type=F32)
            return m_new, l, acc

        init = (jnp.full((tq, 1), NEG_BIG, F32), jnp.zeros((tq, 1), F32), jnp.zeros((tq, A_HEAD_DIM), F32))
        _, l, acc = lax.fori_loop(0, nck, att_chunk, init)
        o_ref[:, h * A_HEAD_DIM:(h + 1) * A_HEAD_DIM] = (acc / l).astype(BF16)


def _attn(proj, kv, g, B, S, tq):
    T = proj.shape[0]
    nq = S // tq
    n_keep = min(TOPK_MAX, S // 4)
    qw = A_HEADS * A_HEAD_DIM
    iw = IDX_HEADS * IDX_DIM
    return pl.pallas_call(
        functools.partial(_attn_body, tq=tq, n_keep=n_keep),
        grid=(B, nq),
        in_specs=[
            pl.BlockSpec((tq, qw), lambda b, i: (b * nq + i, C_AQ // qw)),
            pl.BlockSpec((tq, iw), lambda b, i: (b * nq + i, C_IQ // iw)),
            pl.BlockSpec((S, LANES), lambda b, i: (b, C_IKA // LANES)),
            pl.BlockSpec((S, LANES), lambda b, i: (b, C_IKB // LANES)),
            pl.BlockSpec((S, 2 * A_HEAD_DIM), lambda b, i: (b, 0)),
            pl.BlockSpec((tq, LANES), lambda b, i: (b * nq + i, 0)),
        ],
        out_specs=pl.BlockSpec((tq, qw), lambda b, i: (b * nq + i, 0)),
        out_shape=jax.ShapeDtypeStruct((T, qw), BF16),
        scratch_shapes=[pltpu.VMEM((nq, tq, tq), I32)],
        compiler_params=_cparams(("arbitrary", "arbitrary")),
        name="dsa_attn",
    )(proj, proj, proj, proj, kv, g)


def _mlstm_body(q_ref, k_ref, v_ref, o_ref, gt_ref, y_ref, c_ref, n_ref, m_ref, *, L):
    H = M_HEADS

    @pl.when(pl.program_id(1) == 0)
    def _():
        c_ref[...] = jnp.zeros_like(c_ref)
        n_ref[...] = jnp.zeros_like(n_ref)
        m_ref[...] = jnp.zeros_like(m_ref)

    li = gt_ref[G_MI:G_MI + H, :]
    gf = gt_ref[G_MF:G_MF + H, :]
    lf = jnp.minimum(gf, 0.0) - jnp.log1p(jnp.exp(-jnp.abs(gf)))
    lane = lax.broadcasted_iota(I32, (H, L), 1)
    bcum = lf
    sh = 1
    while sh < L:
        bcum = bcum + jnp.where(lane >= sh, pltpu.roll(bcum, sh, axis=1), 0.0)
        sh *= 2
    b_last = bcum[:, L - 1:L]
    w_end = b_last - bcum + li
    a_end = jnp.max(w_end, axis=-1, keepdims=True)
    wexp = jnp.exp(w_end - a_end)
    m_old = m_ref[:, 0:1]
    m_new = jnp.maximum(b_last + m_old, a_end)
    decay = jnp.exp(b_last + m_old - m_new)
    scl = jnp.exp(a_end - m_new)
    inter_log = bcum + m_old
    crow = li - bcum
    m_ref[...] = jnp.broadcast_to(m_new, m_ref.shape)

    stack = jnp.concatenate([bcum, wexp, inter_log, jnp.zeros((LANES - 3 * H, L), F32)], axis=0)
    cols = stack.T

    tri = lax.broadcasted_iota(I32, (L, L), 1) <= lax.broadcasted_iota(I32, (L, L), 0)
    sc = M_QK_DIM ** -0.5
    for h in range(H):
        qh = q_ref[:, h * M_QK_DIM:(h + 1) * M_QK_DIM]
        kh = k_ref[:, h * M_QK_DIM:(h + 1) * M_QK_DIM]
        vh = v_ref[:, h * M_V_DIM:(h + 1) * M_V_DIM]
        oh = o_ref[:, h * M_V_DIM:(h + 1) * M_V_DIM]
        bcum_c = cols[:, h:h + 1]
        wexp_c = cols[:, H + h:H + h + 1]
        ilog_c = cols[:, 2 * H + h:2 * H + h + 1]
        C = c_ref[h]
        nrow = n_ref[h:h + 1, :]

        inter_num = jnp.dot(qh, C.astype(BF16), preferred_element_type=F32) * sc
        inter_den = jnp.sum(qh.astype(F32) * nrow, axis=-1, keepdims=True) * sc
        qk = lax.dot_general(qh, kh, _NT, preferred_element_type=F32) * sc
        dmat = jnp.where(tri, bcum_c + crow[h:h + 1, :], -jnp.inf)
        m_row = jnp.maximum(ilog_c, jnp.max(dmat, axis=-1, keepdims=True))
        inter_w = jnp.exp(ilog_c - m_row)
        intra_w = jnp.exp(dmat - m_row) * qk
        num = inter_w * inter_num + jnp.dot(intra_w.astype(BF16), vh, preferred_element_type=F32)
        den = inter_w * inter_den + jnp.sum(intra_w, axis=-1, keepdims=True)
        hout = num / jnp.maximum(jnp.abs(den), jnp.exp(-m_row))
        y_ref[:, h * M_V_DIM:(h + 1) * M_V_DIM] = (hout * jax.nn.sigmoid(oh.astype(F32))).astype(BF16)

        kw = kh.astype(F32) * wexp_c
        d_h = decay[h:h + 1, :]
        s_h = scl[h:h + 1, :]
        upd = lax.dot_general(kw.astype(BF16), vh, _TN, preferred_element_type=F32)
        c_ref[h] = d_h * C + s_h * upd
        n_ref[h:h + 1, :] = d_h * nrow + s_h * jnp.sum(kw, axis=0, keepdims=True)


def _mlstm(proj, gt, B, S, L):
    T = proj.shape[0]
    nc = S // L
    qw = M_HEADS * M_QK_DIM
    vw = M_HEADS * M_V_DIM
    return pl.pallas_call(
        functools.partial(_mlstm_body, L=L),
        grid=(B, nc),
        in_specs=[
            pl.BlockSpec((L, qw), lambda b, c: (b * nc + c, C_MQ // qw)),
            pl.BlockSpec((L, qw), lambda b, c: (b * nc + c, C_MK // qw)),
            pl.BlockSpec((L, vw), lambda b, c: (b * nc + c, C_MV // vw)),
            pl.BlockSpec((L, vw), lambda b, c: (b * nc + c, C_MO // vw)),
            pl.BlockSpec((LANES, L), lambda b, c: (0, b * nc + c)),
        ],
        out_specs=pl.BlockSpec((L, vw), lambda b, c: (b * nc + c, 0)),
        out_shape=jax.ShapeDtypeStruct((T, vw), BF16),
        scratch_shapes=[
            pltpu.VMEM((M_HEADS, M_QK_DIM, M_V_DIM), F32),
            pltpu.VMEM((M_HEADS, M_QK_DIM), F32),
            pltpu.VMEM((M_HEADS, LANES), F32),
        ],
        compiler_params=_cparams(("arbitrary", "arbitrary")),
        name="mlstm",
    )(proj, proj, proj, proj, gt)


def _mix_body(ya_ref, ym_ref, wa_ref, wm_ref, ga_ref, gm_ref, o_ref):
    a = jnp.dot(ya_ref[...], wa_ref[...], preferred_element_type=F32)
    m = jnp.dot(ym_ref[...], wm_ref[...], preferred_element_type=F32)
    mix = jax.nn.sigmoid(ga_ref[...].astype(F32)) * a + jax.nn.sigmoid(gm_ref[...].astype(F32)) * m
    o_ref[...] = mix.astype(BF16)


def _mix(ya, ym, wa, wm, proj, tm, tn):
    T = ya.shape[0]
    return pl.pallas_call(
        _mix_body,
        grid=(D_MODEL // tn, T // tm),
        in_specs=[
            pl.BlockSpec((tm, ya.shape[1]), lambda j, i: (i, 0)),
            pl.BlockSpec((tm, ym.shape[1]), lambda j, i: (i, 0)),
            pl.BlockSpec((wa.shape[0], tn), lambda j, i: (0, j)),
            pl.BlockSpec((wm.shape[0], tn), lambda j, i: (0, j)),
            pl.BlockSpec((tm, tn), lambda j, i: (i, C_GA // tn + j)),
            pl.BlockSpec((tm, tn), lambda j, i: (i, C_GM // tn + j)),
        ],
        out_specs=pl.BlockSpec((tm, tn), lambda j, i: (i, j)),
        out_shape=jax.ShapeDtypeStruct((T, D_MODEL), BF16),
        compiler_params=_cparams(("arbitrary", "arbitrary")),
        name="mix",
    )(ya, ym, wa, wm, proj, proj)


def _layer_norm(z, g, b):
    mu = jnp.mean(z, axis=-1, keepdims=True)
    d = z - mu
    var = jnp.mean(d * d, axis=-1, keepdims=True)
    return d * lax.rsqrt(var + LN_EPS) * g + b


def _first_max_index(vals):
    vmax = vals[0]
    for v in vals[1:]:
        vmax = jnp.maximum(vmax, v)
    idx = jnp.full(vmax.shape, len(vals) - 1, I32)
    for k in range(len(vals) - 2, -1, -1):
        idx = jnp.where(vals[k] == vmax, k, idx)
    return idx, vmax


def _ln1_body(mix_ref, h_ref, wo_ref, g_ref, b_ref, wrt_ref, brt_ref, hx_ref, rt_ref, *, tm):
    z = DN_ALPHA * h_ref[...] + jnp.dot(mix_ref[...], wo_ref[...], preferred_element_type=F32)
    h1 = _layer_norm(z, g_ref[...], b_ref[...])
    hx_ref[:, 0:D_MODEL] = h1

    r = lax.dot_general(wrt_ref[...], h1.astype(BF16), _NT, preferred_element_type=F32) + brt_ref[...]
    gl = [r[k:k + 1, :] for k in range(N_GROUPS)]
    g_idx, g_max = _first_max_index(gl)
    g_sum = gl[0] * 0.0
    for v in gl:
        g_sum = g_sum + jnp.exp(v - g_max)
    g_p = 1.0 / g_sum
    el = []
    for e in range(EXPERTS_PER_GROUP):
        rows_e = [N_GROUPS + g * EXPERTS_PER_GROUP + e for g in range(N_GROUPS)]
        v = r[rows_e[-1]:rows_e[-1] + 1, :]
        for g in range(N_GROUPS - 2, -1, -1):
            v = jnp.where(g_idx == g, r[rows_e[g]:rows_e[g] + 1, :], v)
        el.append(v)
    _, e_max = _first_max_index(el)
    ex = [jnp.exp(v - e_max) for v in el]
    e_sum = ex[0] + ex[1] + ex[2] + ex[3]
    ep = [v / e_sum for v in ex]
    i1, p1 = _first_max_index(ep)
    rest = [jnp.where(i1 == e, -1.0, ep[e]) for e in range(EXPERTS_PER_GROUP)]
    i2, p2 = _first_max_index(rest)
    den = p1 + p2
    rows = []
    for e in range(EXPERTS_PER_GROUP):
        w = jnp.where(i1 == e, p1 / den, jnp.where(i2 == e, p2 / den, 0.0))
        rows.append(g_p * w)
    rows.append(g_idx.astype(F32))
    rows.append(jnp.zeros((8 - len(rows), tm), F32))
    route_t = jnp.concatenate(rows, axis=0)
    rt_ref[...] = route_t
    full = jnp.concatenate([route_t, jnp.zeros((LANES - 8, tm), F32)], axis=0)
    hx_ref[:, D_MODEL:HX_W] = full.T


def _ln1(mix, h, wo, g, b, wrt, brt, tm):
    T = mix.shape[0]
    return pl.pallas_call(
        functools.partial(_ln1_body, tm=tm),
        grid=(T // tm,),
        in_specs=[
            pl.BlockSpec((tm, D_MODEL), lambda i: (i, 0)),
            pl.BlockSpec((tm, D_MODEL), lambda i: (i, 0)),
            pl.BlockSpec((D_MODEL, D_MODEL), lambda i: (0, 0)),
            pl.BlockSpec((1, D_MODEL), lambda i: (0, 0)),
            pl.BlockSpec((1, D_MODEL), lambda i: (0, 0)),
            pl.BlockSpec((LANES, D_MODEL), lambda i: (0, 0)),
            pl.BlockSpec((LANES, 1), lambda i: (0, 0)),
        ],
        out_specs=[
            pl.BlockSpec((tm, HX_W), lambda i: (i, 0)),
            pl.BlockSpec((8, tm), lambda i: (0, i)),
        ],
        out_shape=[
            jax.ShapeDtypeStruct((T, HX_W), F32),
            jax.ShapeDtypeStruct((8, T), F32),
        ],
        compiler_params=_cparams(("arbitrary",)),
        name="ln1_router",
    )(mix, h, wo, g, b, wrt, brt)


META_START, META_COUNT, META_TOTAL = 0, N_GROUPS, 2 * N_GROUPS


def _pos_body(gid_ref, pos_ref, meta_ref, *, tile, R):
    gid = gid_ref[...]
    upper = (lax.broadcasted_iota(I32, (LANES, LANES), 0) <= lax.broadcasted_iota(I32, (LANES, LANES), 1))
    upper = jnp.where(upper, 1.0, 0.0).astype(BF16)
    below = lax.broadcasted_iota(I32, (R, R), 1) < lax.broadcasted_iota(I32, (R, R), 0)
    below = jnp.where(below, 1.0, 0.0).astype(BF16)
    lane = lax.broadcasted_iota(I32, (1, LANES), 1)
    pos = jnp.zeros((R, LANES), F32)
    meta = jnp.zeros((1, LANES), F32)
    start = jnp.zeros((1, 1), F32)
    for g in range(N_GROUPS):
        oh = gid == float(g)
        incl = jnp.dot(jnp.where(oh, 1.0, 0.0).astype(BF16), upper, preferred_element_type=F32)
        rowtot = incl[:, LANES - 1:LANES]
        before = jnp.dot(below, jnp.broadcast_to(rowtot, (R, LANES)).astype(BF16), preferred_element_type=F32)
        cnt = jnp.sum(rowtot, axis=0, keepdims=True)
        pos = jnp.where(oh, start + before + incl - 1.0, pos)
        meta = jnp.where(lane == META_START + g, start, meta)
        meta = jnp.where(lane == META_COUNT + g, cnt, meta)
        start = start + jnp.floor((cnt + (tile - 1)) * (1.0 / tile)) * tile
    meta = jnp.where(lane == META_TOTAL, start, meta)
    pos_ref[...] = pos.astype(I32)
    meta_ref[...] = meta.astype(I32)


def _positions(gid, tile):
    R = gid.shape[0]
    return pl.pallas_call(
        functools.partial(_pos_body, tile=tile, R=R),
        out_shape=[jax.ShapeDtypeStruct((R, LANES), I32), jax.ShapeDtypeStruct((1, LANES), I32)],
        name="positions",
    )(gid)


def _row_copy(src, dst, s_row, d_row, sem):
    return pltpu.make_async_copy(src.at[pl.ds(s_row, 1), :], dst.at[pl.ds(d_row, 1), :], sem)


def _scatter_body(pos_ref, meta_ref, src_ref, dst_ref, zero_ref, sem, zsem, *, nb):
    i = pl.program_id(0)
    base = i * nb

    @pl.when(i == 0)
    def _():
        zero_ref[...] = jnp.zeros_like(zero_ref)
        for g in range(N_GROUPS + 1):
            if g < N_GROUPS:
                lo = meta_ref[META_START + g] + meta_ref[META_COUNT + g]
                hi = meta_ref[META_START + g + 1] if g + 1 < N_GROUPS else meta_ref[META_TOTAL]
            else:
                lo, hi = meta_ref[META_TOTAL], dst_ref.shape[0]

            def zstart(r, c):
                pltpu.make_async_copy(zero_ref, dst_ref.at[pl.ds(r, 1), :], zsem).start()
                return c

            def zwait(r, c):
                pltpu.make_async_copy(zero_ref, dst_ref.at[pl.ds(r, 1), :], zsem).wait()
                return c

            lax.fori_loop(lo, hi, zstart, 0)
            lax.fori_loop(lo, hi, zwait, 0)

    def start(t, c):
        _row_copy(src_ref, dst_ref, base + t, pos_ref[base + t], sem).start()
        return c

    def wait(t, c):
        _row_copy(src_ref, dst_ref, base + t, pos_ref[base + t], sem).wait()
        return c

    lax.fori_loop(0, nb, start, 0)
    lax.fori_loop(0, nb, wait, 0)


def _scatter_rows(pos, meta, src, n_out, nb):
    T, W = src.shape
    return pl.pallas_call(
        functools.partial(_scatter_body, nb=nb),
        grid_spec=pltpu.PrefetchScalarGridSpec(
            num_scalar_prefetch=2,
            grid=(T // nb,),
            in_specs=[pl.BlockSpec(memory_space=pl.ANY)],
            out_specs=pl.BlockSpec(memory_space=pl.ANY),
            scratch_shapes=[pltpu.VMEM((1, W), src.dtype), pltpu.SemaphoreType.DMA(()), pltpu.SemaphoreType.DMA(())],
        ),
        out_shape=jax.ShapeDtypeStruct((n_out, W), src.dtype),
        compiler_params=pltpu.CompilerParams(dimension_semantics=("arbitrary",), has_side_effects=True),
        name="scatter_rows",
    )(pos, meta, src)


def _gather_body(pos_ref, src_ref, dst_ref, sem, *, nb):
    base = pl.program_id(0) * nb

    def start(t, c):
        _row_copy(src_ref, dst_ref, pos_ref[base + t], base + t, sem).start()
        return c

    def wait(t, c):
        _row_copy(src_ref, dst_ref, pos_ref[base + t], base + t, sem).wait()
        return c

    lax.fori_loop(0, nb, start, 0)
    lax.fori_loop(0, nb, wait, 0)


def _gather_rows(pos, src, nb):
    T = pos.shape[0]
    W = src.shape[1]
    return pl.pallas_call(
        functools.partial(_gather_body, nb=nb),
        grid_spec=pltpu.PrefetchScalarGridSpec(
            num_scalar_prefetch=1,
            grid=(T // nb,),
            in_specs=[pl.BlockSpec(memory_space=pl.ANY)],
            out_specs=pl.BlockSpec(memory_space=pl.ANY),
            scratch_shapes=[pltpu.SemaphoreType.DMA(())],
        ),
        out_shape=jax.ShapeDtypeStruct((T, W), src.dtype),
        compiler_params=pltpu.CompilerParams(dimension_semantics=("arbitrary",), has_side_effects=True),
        name="gather_rows",
    )(pos, src)


def _tile_group(meta_ref, i, tile):
    row = i * tile
    g = jnp.int32(0)
    for k in range(1, N_GROUPS):
        g = g + (row >= meta_ref[META_START + k]).astype(I32)
    return g


def _moe_body(meta_ref, xs_ref, wg_ref, wu_ref, wd_ref, ys_ref, xb_ref, *, tile):
    i = pl.program_id(0)
    e = pl.program_id(1)
    valid = i * tile < meta_ref[META_TOTAL]

    @pl.when(jnp.logical_and(valid, e == 0))
    def _():
        xb_ref[...] = xs_ref[:, 0:D_MODEL].astype(BF16)

    @pl.when(jnp.logical_and(jnp.logical_not(valid), e == 0))
    def _():
        ys_ref[...] = jnp.zeros_like(ys_ref)

    @pl.when(valid)
    def _():
        x = xb_ref[...]
        hg = jnp.dot(x, wg_ref[0], preferred_element_type=F32)
        hu = jnp.dot(x, wu_ref[0], preferred_element_type=F32)
        route = xs_ref[:, D_MODEL:HX_W]
        lane = lax.broadcasted_iota(I32, route.shape, 1)
        cw = jnp.sum(jnp.where(lane == e, route, 0.0), axis=-1, keepdims=True)
        hmid = (hg * jax.nn.sigmoid(hg)) * hu * cw
        y = jnp.dot(hmid.astype(BF16), wd_ref[0], preferred_element_type=F32)

        @pl.when(e == 0)
        def _():
            ys_ref[...] = y

        @pl.when(e > 0)
        def _():
            ys_ref[...] += y


def _moe(meta, xs, wg, wu, wd, tile):
    n_rows = xs.shape[0]
    nt = n_rows // tile

    def row_map(i, e, meta_ref):
        last = jnp.maximum(meta_ref[META_TOTAL] // tile - 1, 0)
        return (jnp.minimum(i, last), 0)

    def w_map(i, e, meta_ref):
        last = jnp.maximum(meta_ref[META_TOTAL] // tile - 1, 0)
        g = _tile_group(meta_ref, jnp.minimum(i, last), tile)
        return (g * EXPERTS_PER_GROUP + e, 0, 0)

    return pl.pallas_call(
        functools.partial(_moe_body, tile=tile),
        grid_spec=pltpu.PrefetchScalarGridSpec(
            num_scalar_prefetch=1,
            grid=(nt, EXPERTS_PER_GROUP),
            in_specs=[
                pl.BlockSpec((tile, HX_W), row_map),
                pl.BlockSpec((1, D_MODEL, EXPERT_FF), w_map),
                pl.BlockSpec((1, D_MODEL, EXPERT_FF), w_map),
                pl.BlockSpec((1, EXPERT_FF, D_MODEL), w_map),
            ],
            out_specs=pl.BlockSpec((tile, D_MODEL), lambda i, e, meta_ref: (i, 0)),
            scratch_shapes=[pltpu.VMEM((tile, D_MODEL), BF16)],
        ),
        out_shape=jax.ShapeDtypeStruct((n_rows, D_MODEL), F32),
        compiler_params=_cparams(("arbitrary", "arbitrary")),
        name="moe",
    )(meta, xs, wg, wu, wd)


def _ln2_body(h_ref, ff_ref, p_ref, wp_ref, wpg_ref, g_ref, b_ref, o_ref):
    h1 = h_ref[...]
    ple = jnp.dot(p_ref[...].astype(BF16), wp_ref[...], preferred_element_type=F32)
    gate = jnp.dot(h1.astype(BF16), wpg_ref[...], preferred_element_type=F32)
    z = DN_ALPHA * h1 + ff_ref[...] + ple * jax.nn.sigmoid(gate)
    o_ref[...] = _layer_norm(z, g_ref[...], b_ref[...])


def _ln2(hx, ff, p, wple, wpg, g, b, tm):
    T = ff.shape[0]
    return pl.pallas_call(
        _ln2_body,
        grid=(T // tm,),
        in_specs=[
            pl.BlockSpec((tm, D_MODEL), lambda i: (i, 0)),
            pl.BlockSpec((tm, D_MODEL), lambda i: (i, 0)),
            pl.BlockSpec((tm, PLE_DIM), lambda i: (i, 0)),
            pl.BlockSpec((PLE_DIM, D_MODEL), lambda i: (0, 0)),
            pl.BlockSpec((D_MODEL, D_MODEL), lambda i: (0, 0)),
            pl.BlockSpec((1, D_MODEL), lambda i: (0, 0)),
            pl.BlockSpec((1, D_MODEL), lambda i: (0, 0)),
        ],
        out_specs=pl.BlockSpec((tm, D_MODEL), lambda i: (i, 0)),
        out_shape=jax.ShapeDtypeStruct((T, D_MODEL), F32),
        compiler_params=_cparams(("arbitrary",)),
        name="ple_ln2",
    )(hx, ff, p, wple, wpg, g, b)


def _pack_in_weights(w_in_l, m_i_bias_l, m_f_bias_l):
    parts, off = {}, 0
    for name, n in _SPLITS:
        parts[name] = w_in_l[:, off:off + n]
        off += n
    z64 = jnp.zeros((D_MODEL, IDX_DIM), w_in_l.dtype)
    wp = jnp.concatenate(
        [parts["a_q"], parts["m_q"], parts["m_k"], parts["m_v"], parts["m_o"], parts["g_a"], parts["g_m"],
         parts["i_q"], parts["a_ckv"], parts["i_k"], z64, z64, parts["i_k"]], axis=1).astype(BF16)
    pad = jnp.zeros((D_MODEL, LANES - IDX_HEADS - 2 * M_HEADS), w_in_l.dtype)
    ws = jnp.concatenate([parts["i_w"], parts["m_i"], parts["m_f"], pad], axis=1).astype(BF16)
    bias = jnp.concatenate([jnp.zeros((IDX_HEADS,), F32), m_i_bias_l.astype(F32), m_f_bias_l.astype(F32),
                            jnp.zeros((LANES - IDX_HEADS - 2 * M_HEADS,), F32)])
    return wp, ws, ws.T, bias.reshape(1, LANES), bias.reshape(LANES, 1)


def _tiles(T, S):
    return dict(
        inproj_tm=min(1024, T), inproj_tn=1536, row_tm=min(512, T), mix_tn=1024,
        attn_tq=min(256, S), mlstm_l=min(256, S), moe_tile=min(512, T // 4), dma_rows=min(512, T),
    )


def kernel(x, p, w_in, kv_norm, w_ukv, m_i_bias, m_f_bias, w_proj_a, w_proj_m, w_out, ln1_g, ln1_b, w_grp, b_grp,
           w_exp, b_exp, w_gate, w_up, w_down, w_ple, w_ple_gate, ln2_g, ln2_b):
    B, S, D = x.shape
    T = B * S
    assert D == D_MODEL and T % LANES == 0
    cfg = _tiles(T, S)
    h = x.reshape(T, D)
    for l in range(w_in.shape[0]):
        wp, ws, wst, brow, bcol = _pack_in_weights(w_in[l], m_i_bias[l], m_f_bias[l])
        proj, g, gt = _inproj(h, wp, ws, wst, brow, bcol, cfg["inproj_tm"], cfg["inproj_tn"])
        kv = _kv(proj, kv_norm[l].reshape(1, -1).astype(F32), w_ukv[l].astype(BF16), cfg["row_tm"])
        y_a = _attn(proj, kv, g, B, S, cfg["attn_tq"])
        y_m = _mlstm(proj, gt, B, S, cfg["mlstm_l"])
        mix = _mix(y_a, y_m, w_proj_a[l].astype(BF16), w_proj_m[l].astype(BF16), proj, cfg["row_tm"], cfg["mix_tn"])

        n_route = N_GROUPS + N_EXPERTS
        wr = jnp.concatenate([w_grp[l], w_exp[l], jnp.zeros((D, LANES - n_route), F32)], axis=1)
        br = jnp.concatenate([b_grp[l], b_exp[l], jnp.zeros((LANES - n_route,), F32)])
        hx, rt = _ln1(mix, h, w_out[l].astype(BF16), ln1_g[l].reshape(1, D), ln1_b[l].reshape(1, D),
                      wr.T.astype(BF16), br.reshape(LANES, 1), cfg["row_tm"])

        tile = cfg["moe_tile"]
        pos, meta = _positions(rt[N_GROUPS].reshape(T // LANES, LANES), tile)
        pos = pos.reshape(T)
        meta = meta.reshape(LANES)
        xs = _scatter_rows(pos, meta, hx, T + N_GROUPS * tile, cfg["dma_rows"])
        ys = _moe(meta, xs, w_gate[l].astype(BF16), w_up[l].astype(BF16), w_down[l].astype(BF16), tile)
        ff = _gather_rows(pos, ys, cfg["dma_rows"])
        h = _ln2(hx, ff, p[l].reshape(T, PLE_DIM), w_ple[l].astype(BF16), w_ple_gate[l].astype(BF16),
                 ln2_g[l].reshape(1, D), ln2_b[l].reshape(1, D), cfg["row_tm"])
    return h.reshape(B, S, D)
```

```python
import functools

import jax
import jax.numpy as jnp
from jax import lax
from jax.experimental import pallas as pl
from jax.experimental.pallas import tpu as pltpu

F32 = jnp.float32
BF16 = jnp.bfloat16
I32 = jnp.int32

D_MODEL = 2048
DEPTH = 2
A_HEADS = 16
A_HEAD_DIM = 128
A_KV_RANK = 256
IDX_HEADS = 16
IDX_DIM = 64
TOPK_MAX = 256
M_HEADS = 8
M_QK_DIM = 128
M_V_DIM = 256
N_GROUPS = 4
EXPERTS_PER_GROUP = 4
N_EXPERTS = N_GROUPS * EXPERTS_PER_GROUP
EXPERT_FF = 512
PLE_DIM = 256
LN_EPS = 1e-5
DN_ALPHA = (2 * DEPTH) ** 0.25

_SPLITS = (
    ("a_q", A_HEADS * A_HEAD_DIM), ("a_ckv", A_KV_RANK), ("i_q", IDX_HEADS * IDX_DIM), ("i_k", IDX_DIM),
    ("i_w", IDX_HEADS), ("m_q", M_HEADS * M_QK_DIM), ("m_k", M_HEADS * M_QK_DIM), ("m_v", M_HEADS * M_V_DIM),
    ("m_o", M_HEADS * M_V_DIM), ("m_i", M_HEADS), ("m_f", M_HEADS), ("g_a", D_MODEL), ("g_m", D_MODEL),
)

C_AQ, C_MQ, C_MK, C_MV, C_MO, C_GA, C_GM = 0, 2048, 3072, 4096, 6144, 8192, 10240
C_IQ, C_CKV, C_IKA, C_IKB = 12288, 13312, 13568, 13696
N_MAIN = 13824
G_IW, G_MI, G_MF = 0, 16, 24
LANES = 128

INT_MIN = -(2 ** 31)
NEG_BIG = -1e30
VMEM_LIMIT = 56 * 1024 * 1024

_NT = (((1,), (1,)), ((), ()))
_TN = (((0,), (0,)), ((), ()))


def _cparams(sem):
    return pltpu.CompilerParams(dimension_semantics=sem, vmem_limit_bytes=VMEM_LIMIT)


def _inproj_body(x_ref, w_ref, ws_ref, wst_ref, brow_ref, bcol_ref, o_ref, g_ref, gt_ref, xb_ref):
    @pl.when(pl.program_id(1) == 0)
    def _():
        xb = x_ref[...].astype(BF16)
        xb_ref[...] = xb
        g_ref[...] = jnp.dot(xb, ws_ref[...], preferred_element_type=F32) + brow_ref[...]
        gt_ref[...] = lax.dot_general(wst_ref[...], xb, _NT, preferred_element_type=F32) + bcol_ref[...]

    o_ref[...] = jnp.dot(xb_ref[...], w_ref[...], preferred_element_type=F32).astype(BF16)


def _inproj(x, wp, ws, wst, brow, bcol, tm, tn):
    T = x.shape[0]
    return pl.pallas_call(
        _inproj_body,
        grid=(T // tm, N_MAIN // tn),
        in_specs=[
            pl.BlockSpec((tm, D_MODEL), lambda i, j: (i, 0)),
            pl.BlockSpec((D_MODEL, tn), lambda i, j: (0, j)),
            pl.BlockSpec((D_MODEL, LANES), lambda i, j: (0, 0)),
            pl.BlockSpec((LANES, D_MODEL), lambda i, j: (0, 0)),
            pl.BlockSpec((1, LANES), lambda i, j: (0, 0)),
            pl.BlockSpec((LANES, 1), lambda i, j: (0, 0)),
        ],
        out_specs=[
            pl.BlockSpec((tm, tn), lambda i, j: (i, j)),
            pl.BlockSpec((tm, LANES), lambda i, j: (i, 0)),
            pl.BlockSpec((LANES, tm), lambda i, j: (0, i)),
        ],
        out_shape=[
            jax.ShapeDtypeStruct((T, N_MAIN), BF16),
            jax.ShapeDtypeStruct((T, LANES), F32),
            jax.ShapeDtypeStruct((LANES, T), F32),
        ],
        scratch_shapes=[pltpu.VMEM((tm, D_MODEL), BF16)],
        compiler_params=_cparams(("arbitrary", "arbitrary")),
        name="inproj",
    )(x, wp, ws, wst, brow, bcol)


def _kv_body(c_ref, gn_ref, w_ref, o_ref):
    c = c_ref[...].astype(F32)
    r = c * lax.rsqrt(jnp.mean(c * c, axis=-1, keepdims=True) + LN_EPS) * gn_ref[...]
    o_ref[...] = jnp.dot(r.astype(BF16), w_ref[...], preferred_element_type=F32).astype(BF16)


def _kv(proj, gn, wukv, tm):
    T = proj.shape[0]
    return pl.pallas_call(
        _kv_body,
        grid=(T // tm,),
        in_specs=[
            pl.BlockSpec((tm, A_KV_RANK), lambda i: (i, C_CKV // A_KV_RANK)),
            pl.BlockSpec((1, A_KV_RANK), lambda i: (0, 0)),
            pl.BlockSpec((A_KV_RANK, 2 * A_HEAD_DIM), lambda i: (0, 0)),
        ],
        out_specs=pl.BlockSpec((tm, 2 * A_HEAD_DIM), lambda i: (i, 0)),
        out_shape=jax.ShapeDtypeStruct((T, 2 * A_HEAD_DIM), BF16),
        compiler_params=_cparams(("arbitrary",)),
        name="kv",
    )(proj, gn, wukv)


def _attn_body(q_ref, iq_ref, ika_ref, ikb_ref, kv_ref, g_ref, o_ref, key_ref, *, tq, n_keep):
    qi = pl.program_id(1)
    nck = qi + 1
    t0 = qi * tq
    row_t = t0 + lax.broadcasted_iota(I32, (tq, tq), 0)
    col_l = lax.broadcasted_iota(I32, (tq, tq), 1)

    wq = g_ref[:, G_IW:G_IW + IDX_HEADS] * (IDX_HEADS ** -0.5 * IDX_DIM ** -0.5)

    def score_chunk(c, carry):
        k0 = pl.multiple_of(c * tq, tq)
        ka = ika_ref[pl.ds(k0, tq), :]
        kb = ikb_ref[pl.ds(k0, tq), :]
        acc = jnp.zeros((tq, tq), F32)
        for p in range(IDX_HEADS // 2):
            pair = iq_ref[:, p * LANES:(p + 1) * LANES]
            xe = lax.dot_general(pair, ka, _NT, preferred_element_type=F32)
            xo = lax.dot_general(pair, kb, _NT, preferred_element_type=F32)
            acc = acc + wq[:, 2 * p:2 * p + 1] * jnp.maximum(xe, 0.0)
            acc = acc + wq[:, 2 * p + 1:2 * p + 2] * jnp.maximum(xo, 0.0)
        bits = lax.bitcast_convert_type(acc, I32)
        key = bits ^ ((bits >> 31) & 0x7FFFFFFF)
        key_ref[c] = jnp.where(k0 + col_l <= row_t, key, INT_MIN)
        return carry

    lax.fori_loop(0, nck, score_chunk, 0)

    def bisect(b, ans_u):
        cand_u = ans_u | lax.shift_left(jnp.int32(1), 31 - b)
        cand_s = cand_u ^ INT_MIN

        def count_chunk(c, cnt):
            hit = jnp.where(key_ref[c] >= cand_s, 1.0, 0.0)
            for v in range(tq // LANES):
                cnt = cnt + hit[:, v * LANES:(v + 1) * LANES]
            return cnt

        cnt = lax.fori_loop(0, nck, count_chunk, jnp.zeros((tq, LANES), F32))
        tot = jnp.sum(cnt, axis=-1, keepdims=True)
        return jnp.where(tot >= n_keep, cand_u, ans_u)

    ans_u = lax.fori_loop(0, 32, bisect, jnp.zeros((tq, 1), I32))
    tau = jnp.where(row_t[:, :1] < n_keep, INT_MIN + 1, ans_u ^ INT_MIN)

    scale = A_HEAD_DIM ** -0.5
    for h in range(A_HEADS):
        qh = q_ref[:, h * A_HEAD_DIM:(h + 1) * A_HEAD_DIM]
        slope = 2.0 ** (-8.0 * (h + 1) / A_HEADS)

        def att_chunk(c, carry, qh=qh, slope=slope):
            m, l, acc = carry
            k0 = pl.multiple_of(c * tq, tq)
            kc = kv_ref[pl.ds(k0, tq), 0:A_HEAD_DIM]
            vc = kv_ref[pl.ds(k0, tq), A_HEAD_DIM:2 * A_HEAD_DIM]
            s = lax.dot_general(qh, kc, _NT, preferred_element_type=F32)
            rel = (k0 - t0 + lax.broadcasted_iota(I32, (1, tq), 1)).astype(F32)
            msk = key_ref[c] >= tau
            lg = jnp.where(msk, s * scale + slope * rel, NEG_BIG)
            m_new = jnp.maximum(m, jnp.max(lg, axis=-1, keepdims=True))
            alpha = jnp.exp(m - m_new)
            p = jnp.where(msk, jnp.exp(lg - m_new), 0.0)
            l = alpha * l + jnp.sum(p, axis=-1, keepdims=True)
            acc = alpha * acc + jnp.dot(p.astype(BF16), vc, preferred_element_type=F32)
            return m_new, l, acc

        init = (jnp.full((tq, 1), NEG_BIG, F32), jnp.zeros((tq, 1), F32), jnp.zeros((tq, A_HEAD_DIM), F32))
        _, l, acc = lax.fori_loop(0, nck, att_chunk, init)
        o_ref[:, h * A_HEAD_DIM:(h + 1) * A_HEAD_DIM] = (acc / l).astype(BF16)


def _attn(proj, kv, g, B, S, tq):
    T = proj.shape[0]
    nq = S // tq
    n_keep = min(TOPK_MAX, S // 4)
    qw = A_HEADS * A_HEAD_DIM
    iw = IDX_HEADS * IDX_DIM
    return pl.pallas_call(
        functools.partial(_attn_body, tq=tq, n_keep=n_keep),
        grid=(B, nq),
        in_specs=[
            pl.BlockSpec((tq, qw), lambda b, i: (b * nq + i, C_AQ // qw)),
            pl.BlockSpec((tq, iw), lambda b, i: (b * nq + i, C_IQ // iw)),
            pl.BlockSpec((S, LANES), lambda b, i: (b, C_IKA // LANES)),
            pl.BlockSpec((S, LANES), lambda b, i: (b, C_IKB // LANES)),
            pl.BlockSpec((S, 2 * A_HEAD_DIM), lambda b, i: (b, 0)),
            pl.BlockSpec((tq, LANES), lambda b, i: (b * nq + i, 0)),
        ],
        out_specs=pl.BlockSpec((tq, qw), lambda b, i: (b * nq + i, 0)),
        out_shape=jax.ShapeDtypeStruct((T, qw), BF16),
        scratch_shapes=[pltpu.VMEM((nq, tq, tq), I32)],
        compiler_params=_cparams(("arbitrary", "arbitrary")),
        name="dsa_attn",
    )(proj, proj, proj, proj, kv, g)


def _mlstm_body(q_ref, k_ref, v_ref, o_ref, gt_ref, y_ref, c_ref, n_ref, m_ref, *, L):
    H = M_HEADS

    @pl.when(pl.program_id(1) == 0)
    def _():
        c_ref[...] = jnp.zeros_like(c_ref)
        n_ref[...] = jnp.zeros_like(n_ref)
        m_ref[...] = jnp.zeros_like(m_ref)

    li = gt_ref[G_MI:G_MI + H, :]
    gf = gt_ref[G_MF:G_MF + H, :]
    lf = jnp.minimum(gf, 0.0) - jnp.log1p(jnp.exp(-jnp.abs(gf)))
    lane = lax.broadcasted_iota(I32, (H, L), 1)
    bcum = lf
    sh = 1
    while sh < L:
        bcum = bcum + jnp.where(lane >= sh, pltpu.roll(bcum, sh, axis=1), 0.0)
        sh *= 2
    b_last = bcum[:, L - 1:L]
    w_end = b_last - bcum + li
    a_end = jnp.max(w_end, axis=-1, keepdims=True)
    wexp = jnp.exp(w_end - a_end)
    m_old = m_ref[:, 0:1]
    m_new = jnp.maximum(b_last + m_old, a_end)
    decay = jnp.exp(b_last + m_old - m_new)
    scl = jnp.exp(a_end - m_new)
    inter_log = bcum + m_old
    crow = li - bcum
    m_ref[...] = jnp.broadcast_to(m_new, m_ref.shape)

    stack = jnp.concatenate([bcum, wexp, inter_log, jnp.zeros((LANES - 3 * H, L), F32)], axis=0)
    cols = stack.T

    tri = lax.broadcasted_iota(I32, (L, L), 1) <= lax.broadcasted_iota(I32, (L, L), 0)
    sc = M_QK_DIM ** -0.5
    for h in range(H):
        qh = q_ref[:, h * M_QK_DIM:(h + 1) * M_QK_DIM]
        kh = k_ref[:, h * M_QK_DIM:(h + 1) * M_QK_DIM]
        vh = v_ref[:, h * M_V_DIM:(h + 1) * M_V_DIM]
        oh = o_ref[:, h * M_V_DIM:(h + 1) * M_V_DIM]
        bcum_c = cols[:, h:h + 1]
        wexp_c = cols[:, H + h:H + h + 1]
        ilog_c = cols[:, 2 * H + h:2 * H + h + 1]
        C = c_ref[h]
        nrow = n_ref[h:h + 1, :]

        inter_num = jnp.dot(qh, C.astype(BF16), preferred_element_type=F32) * sc
        inter_den = jnp.sum(qh.astype(F32) * nrow, axis=-1, keepdims=True) * sc
        qk = lax.dot_general(qh, kh, _NT, preferred_element_type=F32) * sc
        dmat = jnp.where(tri, bcum_c + crow[h:h + 1, :], -jnp.inf)
        m_row = jnp.maximum(ilog_c, jnp.max(dmat, axis=-1, keepdims=True))
        inter_w = jnp.exp(ilog_c - m_row)
        intra_w = jnp.exp(dmat - m_row) * qk
        num = inter_w * inter_num + jnp.dot(intra_w.astype(BF16), vh, preferred_element_type=F32)
        den = inter_w * inter_den + jnp.sum(intra_w, axis=-1, keepdims=True)
        hout = num / jnp.maximum(jnp.abs(den), jnp.exp(-m_row))
        y_ref[:, h * M_V_DIM:(h + 1) * M_V_DIM] = (hout * jax.nn.sigmoid(oh.astype(F32))).astype(BF16)

        kw = kh.astype(F32) * wexp_c
        d_h = decay[h:h + 1, :]
        s_h = scl[h:h + 1, :]
        upd = lax.dot_general(kw.astype(BF16), vh, _TN, preferred_element_type=F32)
        c_ref[h] = d_h * C + s_h * upd
        n_ref[h:h + 1, :] = d_h * nrow + s_h * jnp.sum(kw, axis=0, keepdims=True)


def _mlstm(proj, gt, B, S, L):
    T = proj.shape[0]
    nc = S // L
    qw = M_HEADS * M_QK_DIM
    vw = M_HEADS * M_V_DIM
    return pl.pallas_call(
        functools.partial(_mlstm_body, L=L),
        grid=(B, nc),
        in_specs=[
            pl.BlockSpec((L, qw), lambda b, c: (b * nc + c, C_MQ // qw)),
            pl.BlockSpec((L, qw), lambda b, c: (b * nc + c, C_MK // qw)),
            pl.BlockSpec((L, vw), lambda b, c: (b * nc + c, C_MV // vw)),
            pl.BlockSpec((L, vw), lambda b, c: (b * nc + c, C_MO // vw)),
            pl.BlockSpec((LANES, L), lambda b, c: (0, b * nc + c)),
        ],
        out_specs=pl.BlockSpec((L, vw), lambda b, c: (b * nc + c, 0)),
        out_shape=jax.ShapeDtypeStruct((T, vw), BF16),
        scratch_shapes=[
            pltpu.VMEM((M_HEADS, M_QK_DIM, M_V_DIM), F32),
            pltpu.VMEM((M_HEADS, M_QK_DIM), F32),
            pltpu.VMEM((M_HEADS, LANES), F32),
        ],
        compiler_params=_cparams(("arbitrary", "arbitrary")),
        name="mlstm",
    )(proj, proj, proj, proj, gt)


def _mix_body(ya_ref, ym_ref, wa_ref, wm_ref, ga_ref, gm_ref, o_ref):
    a = jnp.dot(ya_ref[...], wa_ref[...], preferred_element_type=F32)
    m = jnp.dot(ym_ref[...], wm_ref[...], preferred_element_type=F32)
    mix = jax.nn.sigmoid(ga_ref[...].astype(F32)) * a + jax.nn.sigmoid(gm_ref[...].astype(F32)) * m
    o_ref[...] = mix.astype(BF16)


def _mix(ya, ym, wa, wm, proj, tm, tn):
    T = ya.shape[0]
    return pl.pallas_call(
        _mix_body,
        grid=(D_MODEL // tn, T // tm),
        in_specs=[
            pl.BlockSpec((tm, ya.shape[1]), lambda j, i: (i, 0)),
            pl.BlockSpec((tm, ym.shape[1]), lambda j, i: (i, 0)),
            pl.BlockSpec((wa.shape[0], tn), lambda j, i: (0, j)),
            pl.BlockSpec((wm.shape[0], tn), lambda j, i: (0, j)),
            pl.BlockSpec((tm, tn), lambda j, i: (i, C_GA // tn + j)),
            pl.BlockSpec((tm, tn), lambda j, i: (i, C_GM // tn + j)),
        ],
        out_specs=pl.BlockSpec((tm, tn), lambda j, i: (i, j)),
        out_shape=jax.ShapeDtypeStruct((T, D_MODEL), BF16),
        compiler_params=_cparams(("arbitrary", "arbitrary")),
        name="mix",
    )(ya, ym, wa, wm, proj, proj)


def _layer_norm(z, g, b):
    mu = jnp.mean(z, axis=-1, keepdims=True)
    d = z - mu
    var = jnp.mean(d * d, axis=-1, keepdims=True)
    return d * lax.rsqrt(var + LN_EPS) * g + b


def _first_max_index(vals):
    vmax = vals[0]
    for v in vals[1:]:
        vmax = jnp.maximum(vmax, v)
    idx = jnp.full(vmax.shape, len(vals) - 1, I32)
    for k in range(len(vals) - 2, -1, -1):
        idx = jnp.where(vals[k] == vmax, k, idx)
    return idx, vmax


def _ln1_body(mix_ref, h_ref, wo_ref, g_ref, b_ref, wrt_ref, brt_ref, h1_ref, route_ref, rt_ref, *, tm):
    z = DN_ALPHA * h_ref[...] + jnp.dot(mix_ref[...], wo_ref[...], preferred_element_type=F32)
    h1 = _layer_norm(z, g_ref[...], b_ref[...])
    h1_ref[...] = h1

    r = lax.dot_general(wrt_ref[...], h1.astype(BF16), _NT, preferred_element_type=F32) + brt_ref[...]
    gl = [r[k:k + 1, :] for k in range(N_GROUPS)]
    g_idx, g_max = _first_max_index(gl)
    g_sum = gl[0] * 0.0
    for v in gl:
        g_sum = g_sum + jnp.exp(v - g_max)
    g_p = 1.0 / g_sum
    el = []
    for e in range(EXPERTS_PER_GROUP):
        rows_e = [N_GROUPS + g * EXPERTS_PER_GROUP + e for g in range(N_GROUPS)]
        v = r[rows_e[-1]:rows_e[-1] + 1, :]
        for g in range(N_GROUPS - 2, -1, -1):
            v = jnp.where(g_idx == g, r[rows_e[g]:rows_e[g] + 1, :], v)
        el.append(v)
    _, e_max = _first_max_index(el)
    ex = [jnp.exp(v - e_max) for v in el]
    e_sum = ex[0] + ex[1] + ex[2] + ex[3]
    ep = [v / e_sum for v in ex]
    i1, p1 = _first_max_index(ep)
    rest = [jnp.where(i1 == e, -1.0, ep[e]) for e in range(EXPERTS_PER_GROUP)]
    i2, p2 = _first_max_index(rest)
    den = p1 + p2
    rows = []
    for e in range(EXPERTS_PER_GROUP):
        w = jnp.where(i1 == e, p1 / den, jnp.where(i2 == e, p2 / den, 0.0))
        rows.append(g_p * w)
    rows.append(g_idx.astype(F32))
    rows.append(jnp.zeros((8 - len(rows), tm), F32))
    route_t = jnp.concatenate(rows, axis=0)
    rt_ref[...] = route_t
    full = jnp.concatenate([route_t, jnp.zeros((LANES - 8, tm), F32)], axis=0)
    route_ref[...] = full.T


def _ln1(mix, h, wo, g, b, wrt, brt, tm):
    T = mix.shape[0]
    return pl.pallas_call(
        functools.partial(_ln1_body, tm=tm),
        grid=(T // tm,),
        in_specs=[
            pl.BlockSpec((tm, D_MODEL), lambda i: (i, 0)),
            pl.BlockSpec((tm, D_MODEL), lambda i: (i, 0)),
            pl.BlockSpec((D_MODEL, D_MODEL), lambda i: (0, 0)),
            pl.BlockSpec((1, D_MODEL), lambda i: (0, 0)),
            pl.BlockSpec((1, D_MODEL), lambda i: (0, 0)),
            pl.BlockSpec((LANES, D_MODEL), lambda i: (0, 0)),
            pl.BlockSpec((LANES, 1), lambda i: (0, 0)),
        ],
        out_specs=[
            pl.BlockSpec((tm, D_MODEL), lambda i: (i, 0)),
            pl.BlockSpec((tm, LANES), lambda i: (i, 0)),
            pl.BlockSpec((8, tm), lambda i: (0, i)),
        ],
        out_shape=[
            jax.ShapeDtypeStruct((T, D_MODEL), F32),
            jax.ShapeDtypeStruct((T, LANES), F32),
            jax.ShapeDtypeStruct((8, T), F32),
        ],
        compiler_params=_cparams(("arbitrary",)),
        name="ln1_router",
    )(mix, h, wo, g, b, wrt, brt)


META_START, META_COUNT, META_TOTAL = 0, N_GROUPS, 2 * N_GROUPS


def _pos_body(gid_ref, pos_ref, meta_ref, *, tile, R):
    gid = gid_ref[...]
    upper = (lax.broadcasted_iota(I32, (LANES, LANES), 0) <= lax.broadcasted_iota(I32, (LANES, LANES), 1))
    upper = jnp.where(upper, 1.0, 0.0).astype(BF16)
    below = lax.broadcasted_iota(I32, (R, R), 1) < lax.broadcasted_iota(I32, (R, R), 0)
    below = jnp.where(below, 1.0, 0.0).astype(BF16)
    lane = lax.broadcasted_iota(I32, (1, LANES), 1)
    pos = jnp.zeros((R, LANES), F32)
    meta = jnp.zeros((1, LANES), F32)
    start = jnp.zeros((1, 1), F32)
    for g in range(N_GROUPS):
        oh = gid == float(g)
        incl = jnp.dot(jnp.where(oh, 1.0, 0.0).astype(BF16), upper, preferred_element_type=F32)
        rowtot = incl[:, LANES - 1:LANES]
        before = jnp.dot(below, jnp.broadcast_to(rowtot, (R, LANES)).astype(BF16), preferred_element_type=F32)
        cnt = jnp.sum(rowtot, axis=0, keepdims=True)
        pos = jnp.where(oh, start + before + incl - 1.0, pos)
        meta = jnp.where(lane == META_START + g, start, meta)
        meta = jnp.where(lane == META_COUNT + g, cnt, meta)
        start = start + jnp.floor((cnt + (tile - 1)) * (1.0 / tile)) * tile
    meta = jnp.where(lane == META_TOTAL, start, meta)
    pos_ref[...] = pos.astype(I32)
    meta_ref[...] = meta.astype(I32)


def _positions(gid, tile):
    R = gid.shape[0]
    return pl.pallas_call(
        functools.partial(_pos_body, tile=tile, R=R),
        out_shape=[jax.ShapeDtypeStruct((R, LANES), I32), jax.ShapeDtypeStruct((1, LANES), I32)],
        name="positions",
    )(gid)


SUB = D_MODEL // LANES


def _store_token_major(dst_ref, val, n):
    for s in range(SUB):
        dst_ref[pl.ds(s, n, stride=SUB), :] = val[:, s * LANES:(s + 1) * LANES]


def _token_block(ref, t):
    return ref.at[pl.ds(pl.multiple_of(t * SUB, SUB), SUB), :]


def _scatter_body(pos_ref, meta_ref, h_ref, route_ref, xs_ref, rs_ref, xbuf, zx_ref, zr_ref, xsem, rsem, *, nb):
    i = pl.program_id(0)
    base = i * nb

    @pl.when(i == 0)
    def _():
        zx_ref[...] = jnp.zeros_like(zx_ref)
        zr_ref[...] = jnp.zeros_like(zr_ref)
        for g in range(N_GROUPS + 1):
            if g < N_GROUPS:
                lo = meta_ref[META_START + g] + meta_ref[META_COUNT + g]
                hi = meta_ref[META_START + g + 1] if g + 1 < N_GROUPS else meta_ref[META_TOTAL]
            else:
                lo, hi = meta_ref[META_TOTAL], rs_ref.shape[0]

            def zcopies(r):
                return (pltpu.make_async_copy(zx_ref, _token_block(xs_ref, r), xsem),
                        pltpu.make_async_copy(zr_ref, rs_ref.at[pl.ds(r, 1), :], rsem))

            def zstart(r, c):
                for cp in zcopies(r):
                    cp.start()
                return c

            def zwait(r, c):
                for cp in zcopies(r):
                    cp.wait()
                return c

            lax.fori_loop(lo, hi, zstart, 0)
            lax.fori_loop(lo, hi, zwait, 0)

    _store_token_major(xbuf, h_ref[...], nb)

    def copies(t):
        dst = pos_ref[base + t]
        return (pltpu.make_async_copy(_token_block(xbuf, t), _token_block(xs_ref, dst), xsem),
                pltpu.make_async_copy(route_ref.at[pl.ds(t, 1), :], rs_ref.at[pl.ds(dst, 1), :], rsem))

    def start(t, c):
        for cp in copies(t):
            cp.start()
        return c

    def wait(t, c):
        for cp in copies(t):
            cp.wait()
        return c

    lax.fori_loop(0, nb, start, 0)
    lax.fori_loop(0, nb, wait, 0)


def _scatter_rows(pos, meta, h1, route, n_out, nb):
    T = h1.shape[0]
    return pl.pallas_call(
        functools.partial(_scatter_body, nb=nb),
        grid_spec=pltpu.PrefetchScalarGridSpec(
            num_scalar_prefetch=2,
            grid=(T // nb,),
            in_specs=[
                pl.BlockSpec((nb, D_MODEL), lambda i, pos_ref, meta_ref: (i, 0)),
                pl.BlockSpec((nb, LANES), lambda i, pos_ref, meta_ref: (i, 0)),
            ],
            out_specs=[pl.BlockSpec(memory_space=pl.ANY), pl.BlockSpec(memory_space=pl.ANY)],
            scratch_shapes=[
                pltpu.VMEM((nb * SUB, LANES), F32),
                pltpu.VMEM((SUB, LANES), F32),
                pltpu.VMEM((1, LANES), F32),
                pltpu.SemaphoreType.DMA(()),
                pltpu.SemaphoreType.DMA(()),
            ],
        ),
        out_shape=[jax.ShapeDtypeStruct((n_out * SUB, LANES), F32), jax.ShapeDtypeStruct((n_out, LANES), F32)],
        compiler_params=pltpu.CompilerParams(dimension_semantics=("arbitrary",), has_side_effects=True,
                                             vmem_limit_bytes=VMEM_LIMIT),
        name="scatter_rows",
    )(pos, meta, h1, route)


def _tile_group(meta_ref, i, tile):
    row = i * tile
    g = jnp.int32(0)
    for k in range(1, N_GROUPS):
        g = g + (row >= meta_ref[META_START + k]).astype(I32)
    return g


def _moe_body(meta_ref, xs_ref, rs_ref, wg_ref, wu_ref, wd_ref, ys_ref, xb_ref, acc_ref, *, tile):
    i = pl.program_id(0)
    e = pl.program_id(1)
    last_e = EXPERTS_PER_GROUP - 1
    valid = i * tile < meta_ref[META_TOTAL]

    @pl.when(jnp.logical_and(valid, e == 0))
    def _():
        for s in range(SUB):
            xb_ref[:, s * LANES:(s + 1) * LANES] = xs_ref[pl.ds(s, tile, stride=SUB), :].astype(BF16)

    @pl.when(jnp.logical_and(jnp.logical_not(valid), e == 0))
    def _():
        ys_ref[...] = jnp.zeros_like(ys_ref)

    @pl.when(valid)
    def _():
        x = xb_ref[...]
        hg = jnp.dot(x, wg_ref[0], preferred_element_type=F32)
        hu = jnp.dot(x, wu_ref[0], preferred_element_type=F32)
        route = rs_ref[...]
        lane = lax.broadcasted_iota(I32, route.shape, 1)
        cw = jnp.sum(jnp.where(lane == e, route, 0.0), axis=-1, keepdims=True)
        hmid = (hg * jax.nn.sigmoid(hg)) * hu * cw
        y = jnp.dot(hmid.astype(BF16), wd_ref[0], preferred_element_type=F32)

        @pl.when(e == 0)
        def _():
            acc_ref[...] = y

        @pl.when(jnp.logical_and(e > 0, e < last_e))
        def _():
            acc_ref[...] += y

        @pl.when(e == last_e)
        def _():
            _store_token_major(ys_ref, acc_ref[...] + y, tile)


def _moe(meta, xs, rs, wg, wu, wd, tile):
    n_rows = rs.shape[0]
    nt = n_rows // tile

    def row_map(i, e, meta_ref):
        last = jnp.maximum(meta_ref[META_TOTAL] // tile - 1, 0)
        return (jnp.minimum(i, last), 0)

    def w_map(i, e, meta_ref):
        last = jnp.maximum(meta_ref[META_TOTAL] // tile - 1, 0)
        g = _tile_group(meta_ref, jnp.minimum(i, last), tile)
        return (g * EXPERTS_PER_GROUP + e, 0, 0)

    return pl.pallas_call(
        functools.partial(_moe_body, tile=tile),
        grid_spec=pltpu.PrefetchScalarGridSpec(
            num_scalar_prefetch=1,
            grid=(nt, EXPERTS_PER_GROUP),
            in_specs=[
                pl.BlockSpec((tile * SUB, LANES), row_map),
                pl.BlockSpec((tile, LANES), row_map),
                pl.BlockSpec((1, D_MODEL, EXPERT_FF), w_map),
                pl.BlockSpec((1, D_MODEL, EXPERT_FF), w_map),
                pl.BlockSpec((1, EXPERT_FF, D_MODEL), w_map),
            ],
            out_specs=pl.BlockSpec((tile * SUB, LANES), lambda i, e, meta_ref: (i, 0)),
            scratch_shapes=[pltpu.VMEM((tile, D_MODEL), BF16), pltpu.VMEM((tile, D_MODEL), F32)],
        ),
        out_shape=jax.ShapeDtypeStruct((n_rows * SUB, LANES), F32),
        compiler_params=_cparams(("arbitrary", "arbitrary")),
        name="moe",
    )(meta, xs, rs, wg, wu, wd)


def _ln2_body(pos_ref, h_ref, ys_ref, p_ref, wp_ref, wpg_ref, g_ref, b_ref, o_ref, fbuf, sem, *, tm):
    base = pl.program_id(0) * tm

    def copy(t):
        return pltpu.make_async_copy(_token_block(ys_ref, pos_ref[base + t]), _token_block(fbuf, t), sem)

    def start(t, c):
        copy(t).start()
        return c

    def wait(t, c):
        copy(t).wait()
        return c

    lax.fori_loop(0, tm, start, 0)
    h1 = h_ref[...]
    ple = jnp.dot(p_ref[...].astype(BF16), wp_ref[...], preferred_element_type=F32)
    gate = jnp.dot(h1.astype(BF16), wpg_ref[...], preferred_element_type=F32)
    z = DN_ALPHA * h1 + ple * jax.nn.sigmoid(gate)
    lax.fori_loop(0, tm, wait, 0)
    ff = jnp.concatenate([fbuf[pl.ds(s, tm, stride=SUB), :] for s in range(SUB)], axis=1)
    o_ref[...] = _layer_norm(z + ff, g_ref[...], b_ref[...])


def _ln2(pos, h1, ys, p, wple, wpg, g, b, tm):
    T = h1.shape[0]
    return pl.pallas_call(
        functools.partial(_ln2_body, tm=tm),
        grid_spec=pltpu.PrefetchScalarGridSpec(
            num_scalar_prefetch=1,
            grid=(T // tm,),
            in_specs=[
                pl.BlockSpec((tm, D_MODEL), lambda i, pos_ref: (i, 0)),
                pl.BlockSpec(memory_space=pl.ANY),
                pl.BlockSpec((tm, PLE_DIM), lambda i, pos_ref: (i, 0)),
                pl.BlockSpec((PLE_DIM, D_MODEL), lambda i, pos_ref: (0, 0)),
                pl.BlockSpec((D_MODEL, D_MODEL), lambda i, pos_ref: (0, 0)),
                pl.BlockSpec((1, D_MODEL), lambda i, pos_ref: (0, 0)),
                pl.BlockSpec((1, D_MODEL), lambda i, pos_ref: (0, 0)),
            ],
            out_specs=pl.BlockSpec((tm, D_MODEL), lambda i, pos_ref: (i, 0)),
            scratch_shapes=[pltpu.VMEM((tm * SUB, LANES), F32), pltpu.SemaphoreType.DMA(())],
        ),
        out_shape=jax.ShapeDtypeStruct((T, D_MODEL), F32),
        compiler_params=_cparams(("arbitrary",)),
        name="ple_ln2",
    )(pos, h1, ys, p, wple, wpg, g, b)


def _pack_in_weights(w_in_l, m_i_bias_l, m_f_bias_l):
    parts, off = {}, 0
    for name, n in _SPLITS:
        parts[name] = w_in_l[:, off:off + n]
        off += n
    z64 = jnp.zeros((D_MODEL, IDX_DIM), w_in_l.dtype)
    wp = jnp.concatenate(
        [parts["a_q"], parts["m_q"], parts["m_k"], parts["m_v"], parts["m_o"], parts["g_a"], parts["g_m"],
         parts["i_q"], parts["a_ckv"], parts["i_k"], z64, z64, parts["i_k"]], axis=1).astype(BF16)
    pad = jnp.zeros((D_MODEL, LANES - IDX_HEADS - 2 * M_HEADS), w_in_l.dtype)
    ws = jnp.concatenate([parts["i_w"], parts["m_i"], parts["m_f"], pad], axis=1).astype(BF16)
    bias = jnp.concatenate([jnp.zeros((IDX_HEADS,), F32), m_i_bias_l.astype(F32), m_f_bias_l.astype(F32),
                            jnp.zeros((LANES - IDX_HEADS - 2 * M_HEADS,), F32)])
    return wp, ws, ws.T, bias.reshape(1, LANES), bias.reshape(LANES, 1)


def _tiles(T, S):
    return dict(
        inproj_tm=min(1024, T), inproj_tn=1536, row_tm=min(512, T), mix_tn=1024,
        attn_tq=min(256, S), mlstm_l=min(256, S), moe_tile=min(512, T // 4), dma_rows=min(256, T),
    )


def kernel(x, p, w_in, kv_norm, w_ukv, m_i_bias, m_f_bias, w_proj_a, w_proj_m, w_out, ln1_g, ln1_b, w_grp, b_grp,
           w_exp, b_exp, w_gate, w_up, w_down, w_ple, w_ple_gate, ln2_g, ln2_b):
    B, S, D = x.shape
    T = B * S
    assert D == D_MODEL and T % LANES == 0
    cfg = _tiles(T, S)
    h = x.reshape(T, D)
    for l in range(w_in.shape[0]):
        wp, ws, wst, brow, bcol = _pack_in_weights(w_in[l], m_i_bias[l], m_f_bias[l])
        proj, g, gt = _inproj(h, wp, ws, wst, brow, bcol, cfg["inproj_tm"], cfg["inproj_tn"])
        kv = _kv(proj, kv_norm[l].reshape(1, -1).astype(F32), w_ukv[l].astype(BF16), cfg["row_tm"])
        y_a = _attn(proj, kv, g, B, S, cfg["attn_tq"])
        y_m = _mlstm(proj, gt, B, S, cfg["mlstm_l"])
        mix = _mix(y_a, y_m, w_proj_a[l].astype(BF16), w_proj_m[l].astype(BF16), proj, cfg["row_tm"], cfg["mix_tn"])

        n_route = N_GROUPS + N_EXPERTS
        wr = jnp.concatenate([w_grp[l], w_exp[l], jnp.zeros((D, LANES - n_route), F32)], axis=1)
        br = jnp.concatenate([b_grp[l], b_exp[l], jnp.zeros((LANES - n_route,), F32)])
        h1, route, rt = _ln1(mix, h, w_out[l].astype(BF16), ln1_g[l].reshape(1, D), ln1_b[l].reshape(1, D),
                             wr.T.astype(BF16), br.reshape(LANES, 1), cfg["row_tm"])

        tile = cfg["moe_tile"]
        pos, meta = _positions(rt[N_GROUPS].reshape(T // LANES, LANES), tile)
        pos = pos.reshape(T)
        meta = meta.reshape(LANES)
        xs, rs = _scatter_rows(pos, meta, h1, route, T + N_GROUPS * tile, cfg["dma_rows"])
        ys = _moe(meta, xs, rs, w_gate[l].astype(BF16), w_up[l].astype(BF16), w_down[l].astype(BF16), tile)
        h = _ln2(pos, h1, ys, p[l].reshape(T, PLE_DIM), w_ple[l].astype(BF16), w_ple_gate[l].astype(BF16),
                 ln2_g[l].reshape(1, D), ln2_b[l].reshape(1, D), cfg["row_tm"])
    return h.reshape(B, S, D)
```

```python
import functools

import jax
import jax.numpy as jnp
from jax import lax
from jax.experimental import pallas as pl
from jax.experimental.pallas import tpu as pltpu

F32 = jnp.float32
BF16 = jnp.bfloat16
I32 = jnp.int32

D_MODEL = 2048
DEPTH = 2
A_HEADS = 16
A_HEAD_DIM = 128
A_KV_RANK = 256
IDX_HEADS = 16
IDX_DIM = 64
TOPK_MAX = 256
M_HEADS = 8
M_QK_DIM = 128
M_V_DIM = 256
N_GROUPS = 4
EXPERTS_PER_GROUP = 4
N_EXPERTS = N_GROUPS * EXPERTS_PER_GROUP
EXPERT_FF = 512
PLE_DIM = 256
LN_EPS = 1e-5
DN_ALPHA = (2 * DEPTH) ** 0.25

_SPLITS = (
    ("a_q", A_HEADS * A_HEAD_DIM), ("a_ckv", A_KV_RANK), ("i_q", IDX_HEADS * IDX_DIM), ("i_k", IDX_DIM),
    ("i_w", IDX_HEADS), ("m_q", M_HEADS * M_QK_DIM), ("m_k", M_HEADS * M_QK_DIM), ("m_v", M_HEADS * M_V_DIM),
    ("m_o", M_HEADS * M_V_DIM), ("m_i", M_HEADS), ("m_f", M_HEADS), ("g_a", D_MODEL), ("g_m", D_MODEL),
)

C_AQ, C_MQ, C_MK, C_MV, C_MO, C_GA, C_GM = 0, 2048, 3072, 4096, 6144, 8192, 10240
C_IQ, C_CKV, C_IKA, C_IKB = 12288, 13312, 13568, 13696
N_MAIN = 13824
G_IW, G_MI, G_MF = 0, 16, 24
LANES = 128

INT_MIN = -(2 ** 31)
NEG_BIG = -1e30
LOG2_E = 1.4426950408889634
VMEM_LIMIT = 56 * 1024 * 1024

_NT = (((1,), (1,)), ((), ()))
_TN = (((0,), (0,)), ((), ()))


def _cparams(sem):
    return pltpu.CompilerParams(dimension_semantics=sem, vmem_limit_bytes=VMEM_LIMIT)


def _inproj_body(x_ref, w_ref, ws_ref, wst_ref, brow_ref, bcol_ref, o_ref, g_ref, gt_ref, xb_ref):
    @pl.when(pl.program_id(1) == 0)
    def _():
        xb = x_ref[...].astype(BF16)
        xb_ref[...] = xb
        g_ref[...] = jnp.dot(xb, ws_ref[...], preferred_element_type=F32) + brow_ref[...]
        gt_ref[...] = lax.dot_general(wst_ref[...], xb, _NT, preferred_element_type=F32) + bcol_ref[...]

    o_ref[...] = jnp.dot(xb_ref[...], w_ref[...], preferred_element_type=F32).astype(BF16)


def _inproj(x, wp, ws, wst, brow, bcol, tm, tn):
    T = x.shape[0]
    return pl.pallas_call(
        _inproj_body,
        grid=(T // tm, N_MAIN // tn),
        in_specs=[
            pl.BlockSpec((tm, D_MODEL), lambda i, j: (i, 0)),
            pl.BlockSpec((D_MODEL, tn), lambda i, j: (0, j)),
            pl.BlockSpec((D_MODEL, LANES), lambda i, j: (0, 0)),
            pl.BlockSpec((LANES, D_MODEL), lambda i, j: (0, 0)),
            pl.BlockSpec((1, LANES), lambda i, j: (0, 0)),
            pl.BlockSpec((LANES, 1), lambda i, j: (0, 0)),
        ],
        out_specs=[
            pl.BlockSpec((tm, tn), lambda i, j: (i, j)),
            pl.BlockSpec((tm, LANES), lambda i, j: (i, 0)),
            pl.BlockSpec((LANES, tm), lambda i, j: (0, i)),
        ],
        out_shape=[
            jax.ShapeDtypeStruct((T, N_MAIN), BF16),
            jax.ShapeDtypeStruct((T, LANES), F32),
            jax.ShapeDtypeStruct((LANES, T), F32),
        ],
        scratch_shapes=[pltpu.VMEM((tm, D_MODEL), BF16)],
        compiler_params=_cparams(("arbitrary", "arbitrary")),
        name="inproj",
    )(x, wp, ws, wst, brow, bcol)


def _kv_body(c_ref, gn_ref, w_ref, o_ref):
    c = c_ref[...].astype(F32)
    r = c * lax.rsqrt(jnp.mean(c * c, axis=-1, keepdims=True) + LN_EPS) * gn_ref[...]
    o_ref[...] = jnp.dot(r.astype(BF16), w_ref[...], preferred_element_type=F32).astype(BF16)


def _kv(proj, gn, wukv, tm):
    T = proj.shape[0]
    return pl.pallas_call(
        _kv_body,
        grid=(T // tm,),
        in_specs=[
            pl.BlockSpec((tm, A_KV_RANK), lambda i: (i, C_CKV // A_KV_RANK)),
            pl.BlockSpec((1, A_KV_RANK), lambda i: (0, 0)),
            pl.BlockSpec((A_KV_RANK, 2 * A_HEAD_DIM), lambda i: (0, 0)),
        ],
        out_specs=pl.BlockSpec((tm, 2 * A_HEAD_DIM), lambda i: (i, 0)),
        out_shape=jax.ShapeDtypeStruct((T, 2 * A_HEAD_DIM), BF16),
        compiler_params=_cparams(("arbitrary",)),
        name="kv",
    )(proj, gn, wukv)


def _attn_body(q_ref, iq_ref, ika_ref, ikb_ref, kv_ref, g_ref, o_ref, key_ref, mb_ref, m_ref, acc_ref, *,
               tq, n_keep):
    qi = pl.program_id(1)
    nck = qi + 1
    t0 = qi * tq
    row_t = t0 + lax.broadcasted_iota(I32, (tq, tq), 0)
    col_l = lax.broadcasted_iota(I32, (tq, tq), 1)

    wq = g_ref[:, G_IW:G_IW + IDX_HEADS] * (IDX_HEADS ** -0.5 * IDX_DIM ** -0.5)

    def score_chunk(c, carry):
        k0 = pl.multiple_of(c * tq, tq)
        ka = ika_ref[pl.ds(k0, tq), :]
        kb = ikb_ref[pl.ds(k0, tq), :]
        acc = jnp.zeros((tq, tq), F32)
        for p in range(IDX_HEADS // 2):
            pair = iq_ref[:, p * LANES:(p + 1) * LANES]
            xe = lax.dot_general(pair, ka, _NT, preferred_element_type=F32)
            xo = lax.dot_general(pair, kb, _NT, preferred_element_type=F32)
            acc = acc + wq[:, 2 * p:2 * p + 1] * jnp.maximum(xe, 0.0)
            acc = acc + wq[:, 2 * p + 1:2 * p + 2] * jnp.maximum(xo, 0.0)
        bits = lax.bitcast_convert_type(acc, I32)
        key = bits ^ ((bits >> 31) & 0x7FFFFFFF)
        key_ref[c] = jnp.where(k0 + col_l <= row_t, key, INT_MIN)
        return carry

    lax.fori_loop(0, nck, score_chunk, 0)

    def bisect(b, ans_u):
        cand_u = ans_u | lax.shift_left(jnp.int32(1), 31 - b)
        cand_s = cand_u ^ INT_MIN

        cand_b = jnp.broadcast_to(cand_s, (tq, LANES))

        def count_chunk(c, cnt):
            k = key_ref[c]
            for v in range(tq // LANES):
                cnt = cnt + jnp.where(k[:, v * LANES:(v + 1) * LANES] >= cand_b, 1.0, 0.0)
            return cnt

        cnt = lax.fori_loop(0, nck, count_chunk, jnp.zeros((tq, LANES), F32))
        tot = jnp.sum(cnt, axis=-1, keepdims=True)
        return jnp.where(tot >= n_keep, cand_u, ans_u)

    ans_u = lax.fori_loop(0, 32, bisect, jnp.zeros((tq, 1), I32))
    tau = jnp.where(row_t[:, :1] < n_keep, INT_MIN + 1, ans_u ^ INT_MIN)

    def mask_chunk(c, carry):
        mb_ref[c] = jnp.where(key_ref[c] >= tau, 0.0, NEG_BIG)
        return carry

    lax.fori_loop(0, nck, mask_chunk, 0)

    scale = A_HEAD_DIM ** -0.5
    c_exp = scale * LOG2_E
    m_ref[...] = jnp.full(m_ref.shape, NEG_BIG, F32)
    acc_ref[...] = jnp.zeros(acc_ref.shape, F32)
    rows = min(tq, LANES)
    ones = jnp.ones((tq, A_HEAD_DIM), BF16)

    def att_chunk(c, carry):
        k0 = pl.multiple_of(c * tq, tq)
        kc = kv_ref[pl.ds(k0, tq), 0:A_HEAD_DIM]
        v1 = jnp.concatenate([kv_ref[pl.ds(k0, tq), A_HEAD_DIM:2 * A_HEAD_DIM], ones], axis=1)
        rel = (k0 - t0 + lax.broadcasted_iota(I32, (1, tq), 1)).astype(F32)
        for h in range(A_HEADS):
            srow = (2.0 ** (-8.0 * (h + 1) / A_HEADS) / scale) * rel
            for r in range(tq // rows):
                rs = slice(r * rows, (r + 1) * rows)
                qh = q_ref[rs, h * A_HEAD_DIM:(h + 1) * A_HEAD_DIM]
                u = lax.dot_general(qh, kc, _NT, preferred_element_type=F32) + mb_ref[c, rs, :] + srow
                um = u[:, 0:LANES]
                for v in range(1, tq // LANES):
                    um = jnp.maximum(um, u[:, v * LANES:(v + 1) * LANES])
                m = m_ref[h, rs, :]
                m_new = jnp.maximum(m, jnp.max(um, axis=-1, keepdims=True))
                alpha = jnp.exp2((m - m_new) * c_exp)
                p = jnp.exp2((u - m_new) * c_exp)
                m_ref[h, rs, :] = m_new
                acc_ref[h, rs, :] = alpha * acc_ref[h, rs, :] + jnp.dot(p.astype(BF16), v1,
                                                                        preferred_element_type=F32)
        return carry

    lax.fori_loop(0, nck, att_chunk, 0)
    for h in range(A_HEADS):
        num = acc_ref[h, :, 0:A_HEAD_DIM]
        den = acc_ref[h, :, A_HEAD_DIM:2 * A_HEAD_DIM]
        o_ref[:, h * A_HEAD_DIM:(h + 1) * A_HEAD_DIM] = (num / den).astype(BF16)


def _attn(proj, kv, g, B, S, tq):
    T = proj.shape[0]
    nq = S // tq
    n_keep = min(TOPK_MAX, S // 4)
    qw = A_HEADS * A_HEAD_DIM
    iw = IDX_HEADS * IDX_DIM
    return pl.pallas_call(
        functools.partial(_attn_body, tq=tq, n_keep=n_keep),
        grid=(B, nq),
        in_specs=[
            pl.BlockSpec((tq, qw), lambda b, i: (b * nq + i, C_AQ // qw)),
            pl.BlockSpec((tq, iw), lambda b, i: (b * nq + i, C_IQ // iw)),
            pl.BlockSpec((S, LANES), lambda b, i: (b, C_IKA // LANES)),
            pl.BlockSpec((S, LANES), lambda b, i: (b, C_IKB // LANES)),
            pl.BlockSpec((S, 2 * A_HEAD_DIM), lambda b, i: (b, 0)),
            pl.BlockSpec((tq, LANES), lambda b, i: (b * nq + i, 0)),
        ],
        out_specs=pl.BlockSpec((tq, qw), lambda b, i: (b * nq + i, 0)),
        out_shape=jax.ShapeDtypeStruct((T, qw), BF16),
        scratch_shapes=[
            pltpu.VMEM((nq, tq, tq), I32),
            pltpu.VMEM((nq, tq, tq), F32),
            pltpu.VMEM((A_HEADS, tq, 1), F32),
            pltpu.VMEM((A_HEADS, tq, 2 * A_HEAD_DIM), F32),
        ],
        compiler_params=_cparams(("arbitrary", "arbitrary")),
        name="dsa_attn",
    )(proj, proj, proj, proj, kv, g)


def _mlstm_body(q_ref, k_ref, v_ref, o_ref, gt_ref, y_ref, c_ref, n_ref, m_ref, *, L):
    H = M_HEADS

    @pl.when(pl.program_id(1) == 0)
    def _():
        c_ref[...] = jnp.zeros_like(c_ref)
        n_ref[...] = jnp.zeros_like(n_ref)
        m_ref[...] = jnp.zeros_like(m_ref)

    li = gt_ref[G_MI:G_MI + H, :]
    gf = gt_ref[G_MF:G_MF + H, :]
    lf = jnp.minimum(gf, 0.0) - jnp.log1p(jnp.exp(-jnp.abs(gf)))
    lane = lax.broadcasted_iota(I32, (H, L), 1)
    bcum = lf
    sh = 1
    while sh < L:
        bcum = bcum + jnp.where(lane >= sh, pltpu.roll(bcum, sh, axis=1), 0.0)
        sh *= 2
    b_last = bcum[:, L - 1:L]
    w_end = b_last - bcum + li
    a_end = jnp.max(w_end, axis=-1, keepdims=True)
    wexp = jnp.exp(w_end - a_end)
    m_old = m_ref[:, 0:1]
    m_new = jnp.maximum(b_last + m_old, a_end)
    decay = jnp.exp(b_last + m_old - m_new)
    scl = jnp.exp(a_end - m_new)
    inter_log = bcum + m_old
    crow = li - bcum
    m_ref[...] = jnp.broadcast_to(m_new, m_ref.shape)

    stack = jnp.concatenate([bcum, wexp, inter_log, jnp.zeros((LANES - 3 * H, L), F32)], axis=0)
    cols = stack.T

    tri = lax.broadcasted_iota(I32, (L, L), 1) <= lax.broadcasted_iota(I32, (L, L), 0)
    sc = M_QK_DIM ** -0.5
    for h in range(H):
        qh = q_ref[:, h * M_QK_DIM:(h + 1) * M_QK_DIM]
        kh = k_ref[:, h * M_QK_DIM:(h + 1) * M_QK_DIM]
        vh = v_ref[:, h * M_V_DIM:(h + 1) * M_V_DIM]
        oh = o_ref[:, h * M_V_DIM:(h + 1) * M_V_DIM]
        bcum_c = cols[:, h:h + 1]
        wexp_c = cols[:, H + h:H + h + 1]
        ilog_c = cols[:, 2 * H + h:2 * H + h + 1]
        C = c_ref[h]
        nrow = n_ref[h:h + 1, :]

        inter_num = jnp.dot(qh, C.astype(BF16), preferred_element_type=F32) * sc
        inter_den = jnp.sum(qh.astype(F32) * nrow, axis=-1, keepdims=True) * sc
        qk = lax.dot_general(qh, kh, _NT, preferred_element_type=F32) * sc
        dmat = jnp.where(tri, bcum_c + crow[h:h + 1, :], -jnp.inf)
        m_row = jnp.maximum(ilog_c, jnp.max(dmat, axis=-1, keepdims=True))
        inter_w = jnp.exp(ilog_c - m_row)
        intra_w = jnp.exp(dmat - m_row) * qk
        num = inter_w * inter_num + jnp.dot(intra_w.astype(BF16), vh, preferred_element_type=F32)
        den = inter_w * inter_den + jnp.sum(intra_w, axis=-1, keepdims=True)
        hout = num / jnp.maximum(jnp.abs(den), jnp.exp(-m_row))
        y_ref[:, h * M_V_DIM:(h + 1) * M_V_DIM] = (hout * jax.nn.sigmoid(oh.astype(F32))).astype(BF16)

        kw = kh.astype(F32) * wexp_c
        d_h = decay[h:h + 1, :]
        s_h = scl[h:h + 1, :]
        upd = lax.dot_general(kw.astype(BF16), vh, _TN, preferred_element_type=F32)
        c_ref[h] = d_h * C + s_h * upd
        n_ref[h:h + 1, :] = d_h * nrow + s_h * jnp.sum(kw, axis=0, keepdims=True)


def _mlstm(proj, gt, B, S, L):
    T = proj.shape[0]
    nc = S // L
    qw = M_HEADS * M_QK_DIM
    vw = M_HEADS * M_V_DIM
    return pl.pallas_call(
        functools.partial(_mlstm_body, L=L),
        grid=(B, nc),
        in_specs=[
            pl.BlockSpec((L, qw), lambda b, c: (b * nc + c, C_MQ // qw)),
            pl.BlockSpec((L, qw), lambda b, c: (b * nc + c, C_MK // qw)),
            pl.BlockSpec((L, vw), lambda b, c: (b * nc + c, C_MV // vw)),
            pl.BlockSpec((L, vw), lambda b, c: (b * nc + c, C_MO // vw)),
            pl.BlockSpec((LANES, L), lambda b, c: (0, b * nc + c)),
        ],
        out_specs=pl.BlockSpec((L, vw), lambda b, c: (b * nc + c, 0)),
        out_shape=jax.ShapeDtypeStruct((T, vw), BF16),
        scratch_shapes=[
            pltpu.VMEM((M_HEADS, M_QK_DIM, M_V_DIM), F32),
            pltpu.VMEM((M_HEADS, M_QK_DIM), F32),
            pltpu.VMEM((M_HEADS, LANES), F32),
        ],
        compiler_params=_cparams(("arbitrary", "arbitrary")),
        name="mlstm",
    )(proj, proj, proj, proj, gt)


def _mix_body(ya_ref, ym_ref, wa_ref, wm_ref, ga_ref, gm_ref, o_ref):
    a = jnp.dot(ya_ref[...], wa_ref[...], preferred_element_type=F32)
    m = jnp.dot(ym_ref[...], wm_ref[...], preferred_element_type=F32)
    mix = jax.nn.sigmoid(ga_ref[...].astype(F32)) * a + jax.nn.sigmoid(gm_ref[...].astype(F32)) * m
    o_ref[...] = mix.astype(BF16)


def _mix(ya, ym, wa, wm, proj, tm, tn):
    T = ya.shape[0]
    return pl.pallas_call(
        _mix_body,
        grid=(D_MODEL // tn, T // tm),
        in_specs=[
            pl.BlockSpec((tm, ya.shape[1]), lambda j, i: (i, 0)),
            pl.BlockSpec((tm, ym.shape[1]), lambda j, i: (i, 0)),
            pl.BlockSpec((wa.shape[0], tn), lambda j, i: (0, j)),
            pl.BlockSpec((wm.shape[0], tn), lambda j, i: (0, j)),
            pl.BlockSpec((tm, tn), lambda j, i: (i, C_GA // tn + j)),
            pl.BlockSpec((tm, tn), lambda j, i: (i, C_GM // tn + j)),
        ],
        out_specs=pl.BlockSpec((tm, tn), lambda j, i: (i, j)),
        out_shape=jax.ShapeDtypeStruct((T, D_MODEL), BF16),
        compiler_params=_cparams(("arbitrary", "arbitrary")),
        name="mix",
    )(ya, ym, wa, wm, proj, proj)


def _layer_norm(z, g, b):
    mu = jnp.mean(z, axis=-1, keepdims=True)
    d = z - mu
    var = jnp.mean(d * d, axis=-1, keepdims=True)
    return d * lax.rsqrt(var + LN_EPS) * g + b


def _first_max_index(vals):
    vmax = vals[0]
    for v in vals[1:]:
        vmax = jnp.maximum(vmax, v)
    idx = jnp.full(vmax.shape, len(vals) - 1, I32)
    for k in range(len(vals) - 2, -1, -1):
        idx = jnp.where(vals[k] == vmax, k, idx)
    return idx, vmax


def _ln1_body(mix_ref, h_ref, wo_ref, g_ref, b_ref, wrt_ref, brt_ref, h1_ref, route_ref, rt_ref, *, tm):
    z = DN_ALPHA * h_ref[...] + jnp.dot(mix_ref[...], wo_ref[...], preferred_element_type=F32)
    h1 = _layer_norm(z, g_ref[...], b_ref[...])
    h1_ref[...] = h1

    r = lax.dot_general(wrt_ref[...], h1.astype(BF16), _NT, preferred_element_type=F32) + brt_ref[...]
    gl = [r[k:k + 1, :] for k in range(N_GROUPS)]
    g_idx, g_max = _first_max_index(gl)
    g_sum = gl[0] * 0.0
    for v in gl:
        g_sum = g_sum + jnp.exp(v - g_max)
    g_p = 1.0 / g_sum
    el = []
    for e in range(EXPERTS_PER_GROUP):
        rows_e = [N_GROUPS + g * EXPERTS_PER_GROUP + e for g in range(N_GROUPS)]
        v = r[rows_e[-1]:rows_e[-1] + 1, :]
        for g in range(N_GROUPS - 2, -1, -1):
            v = jnp.where(g_idx == g, r[rows_e[g]:rows_e[g] + 1, :], v)
        el.append(v)
    _, e_max = _first_max_index(el)
    ex = [jnp.exp(v - e_max) for v in el]
    e_sum = ex[0] + ex[1] + ex[2] + ex[3]
    ep = [v / e_sum for v in ex]
    i1, p1 = _first_max_index(ep)
    rest = [jnp.where(i1 == e, -1.0, ep[e]) for e in range(EXPERTS_PER_GROUP)]
    i2, p2 = _first_max_index(rest)
    den = p1 + p2
    rows = []
    for e in range(EXPERTS_PER_GROUP):
        w = jnp.where(i1 == e, p1 / den, jnp.where(i2 == e, p2 / den, 0.0))
        rows.append(g_p * w)
    rows.append(g_idx.astype(F32))
    rows.append(jnp.zeros((8 - len(rows), tm), F32))
    route_t = jnp.concatenate(rows, axis=0)
    rt_ref[...] = route_t
    full = jnp.concatenate([route_t, jnp.zeros((LANES - 8, tm), F32)], axis=0)
    route_ref[...] = full.T


def _ln1(mix, h, wo, g, b, wrt, brt, tm):
    T = mix.shape[0]
    return pl.pallas_call(
        functools.partial(_ln1_body, tm=tm),
        grid=(T // tm,),
        in_specs=[
            pl.BlockSpec((tm, D_MODEL), lambda i: (i, 0)),
            pl.BlockSpec((tm, D_MODEL), lambda i: (i, 0)),
            pl.BlockSpec((D_MODEL, D_MODEL), lambda i: (0, 0)),
            pl.BlockSpec((1, D_MODEL), lambda i: (0, 0)),
            pl.BlockSpec((1, D_MODEL), lambda i: (0, 0)),
            pl.BlockSpec((LANES, D_MODEL), lambda i: (0, 0)),
            pl.BlockSpec((LANES, 1), lambda i: (0, 0)),
        ],
        out_specs=[
            pl.BlockSpec((tm, D_MODEL), lambda i: (i, 0)),
            pl.BlockSpec((tm, LANES), lambda i: (i, 0)),
            pl.BlockSpec((8, tm), lambda i: (0, i)),
        ],
        out_shape=[
            jax.ShapeDtypeStruct((T, D_MODEL), F32),
            jax.ShapeDtypeStruct((T, LANES), F32),
            jax.ShapeDtypeStruct((8, T), F32),
        ],
        compiler_params=_cparams(("arbitrary",)),
        name="ln1_router",
    )(mix, h, wo, g, b, wrt, brt)


META_START, META_COUNT, META_TOTAL = 0, N_GROUPS, 2 * N_GROUPS


def _pos_body(gid_ref, pos_ref, meta_ref, *, tile, R):
    gid = gid_ref[...]
    upper = (lax.broadcasted_iota(I32, (LANES, LANES), 0) <= lax.broadcasted_iota(I32, (LANES, LANES), 1))
    upper = jnp.where(upper, 1.0, 0.0).astype(BF16)
    below = lax.broadcasted_iota(I32, (R, R), 1) < lax.broadcasted_iota(I32, (R, R), 0)
    below = jnp.where(below, 1.0, 0.0).astype(BF16)
    lane = lax.broadcasted_iota(I32, (1, LANES), 1)
    pos = jnp.zeros((R, LANES), F32)
    meta = jnp.zeros((1, LANES), F32)
    start = jnp.zeros((1, 1), F32)
    for g in range(N_GROUPS):
        oh = gid == float(g)
        incl = jnp.dot(jnp.where(oh, 1.0, 0.0).astype(BF16), upper, preferred_element_type=F32)
        rowtot = incl[:, LANES - 1:LANES]
        before = jnp.dot(below, jnp.broadcast_to(rowtot, (R, LANES)).astype(BF16), preferred_element_type=F32)
        cnt = jnp.sum(rowtot, axis=0, keepdims=True)
        pos = jnp.where(oh, start + before + incl - 1.0, pos)
        meta = jnp.where(lane == META_START + g, start, meta)
        meta = jnp.where(lane == META_COUNT + g, cnt, meta)
        start = start + jnp.floor((cnt + (tile - 1)) * (1.0 / tile)) * tile
    meta = jnp.where(lane == META_TOTAL, start, meta)
    pos_ref[...] = pos.astype(I32)
    meta_ref[...] = meta.astype(I32)


def _positions(gid, tile):
    R = gid.shape[0]
    return pl.pallas_call(
        functools.partial(_pos_body, tile=tile, R=R),
        out_shape=[jax.ShapeDtypeStruct((R, LANES), I32), jax.ShapeDtypeStruct((1, LANES), I32)],
        name="positions",
    )(gid)


SUB = D_MODEL // LANES


def _store_token_major(dst_ref, val, n):
    for s in range(SUB):
        dst_ref[pl.ds(s, n, stride=SUB), :] = val[:, s * LANES:(s + 1) * LANES]


def _token_block(ref, t):
    return ref.at[pl.ds(pl.multiple_of(t * SUB, SUB), SUB), :]


def _scatter_body(pos_ref, meta_ref, h_ref, route_ref, xs_ref, rs_ref, xbuf, zx_ref, zr_ref, xsem, rsem, *, nb):
    i = pl.program_id(0)
    base = i * nb

    @pl.when(i == 0)
    def _():
        zx_ref[...] = jnp.zeros_like(zx_ref)
        zr_ref[...] = jnp.zeros_like(zr_ref)
        for g in range(N_GROUPS + 1):
            if g < N_GROUPS:
                lo = meta_ref[META_START + g] + meta_ref[META_COUNT + g]
                hi = meta_ref[META_START + g + 1] if g + 1 < N_GROUPS else meta_ref[META_TOTAL]
            else:
                lo, hi = meta_ref[META_TOTAL], rs_ref.shape[0]

            def zcopies(r):
                return (pltpu.make_async_copy(zx_ref, _token_block(xs_ref, r), xsem),
                        pltpu.make_async_copy(zr_ref, rs_ref.at[pl.ds(r, 1), :], rsem))

            def zstart(r, c):
                for cp in zcopies(r):
                    cp.start()
                return c

            def zwait(r, c):
                for cp in zcopies(r):
                    cp.wait()
                return c

            lax.fori_loop(lo, hi, zstart, 0)
            lax.fori_loop(lo, hi, zwait, 0)

    _store_token_major(xbuf, h_ref[...], nb)

    def copies(t):
        dst = pos_ref[base + t]
        return (pltpu.make_async_copy(_token_block(xbuf, t), _token_block(xs_ref, dst), xsem),
                pltpu.make_async_copy(route_ref.at[pl.ds(t, 1), :], rs_ref.at[pl.ds(dst, 1), :], rsem))

    def start(t, c):
        for cp in copies(t):
            cp.start()
        return c

    def wait(t, c):
        for cp in copies(t):
            cp.wait()
        return c

    lax.fori_loop(0, nb, start, 0)
    lax.fori_loop(0, nb, wait, 0)


def _scatter_rows(pos, meta, h1, route, n_out, nb):
    T = h1.shape[0]
    return pl.pallas_call(
        functools.partial(_scatter_body, nb=nb),
        grid_spec=pltpu.PrefetchScalarGridSpec(
            num_scalar_prefetch=2,
            grid=(T // nb,),
            in_specs=[
                pl.BlockSpec((nb, D_MODEL), lambda i, pos_ref, meta_ref: (i, 0)),
                pl.BlockSpec((nb, LANES), lambda i, pos_ref, meta_ref: (i, 0)),
            ],
            out_specs=[pl.BlockSpec(memory_space=pl.ANY), pl.BlockSpec(memory_space=pl.ANY)],
            scratch_shapes=[
                pltpu.VMEM((nb * SUB, LANES), F32),
                pltpu.VMEM((SUB, LANES), F32),
                pltpu.VMEM((1, LANES), F32),
                pltpu.SemaphoreType.DMA(()),
                pltpu.SemaphoreType.DMA(()),
            ],
        ),
        out_shape=[jax.ShapeDtypeStruct((n_out * SUB, LANES), F32), jax.ShapeDtypeStruct((n_out, LANES), F32)],
        compiler_params=pltpu.CompilerParams(dimension_semantics=("arbitrary",), has_side_effects=True,
                                             vmem_limit_bytes=VMEM_LIMIT),
        name="scatter_rows",
    )(pos, meta, h1, route)


def _tile_group(meta_ref, i, tile):
    row = i * tile
    g = jnp.int32(0)
    for k in range(1, N_GROUPS):
        g = g + (row >= meta_ref[META_START + k]).astype(I32)
    return g


def _moe_body(meta_ref, xs_ref, rs_ref, wg_ref, wu_ref, wd_ref, ys_ref, xb_ref, acc_ref, *, tile):
    i = pl.program_id(0)
    e = pl.program_id(1)
    last_e = EXPERTS_PER_GROUP - 1
    valid = i * tile < meta_ref[META_TOTAL]

    @pl.when(jnp.logical_and(valid, e == 0))
    def _():
        for s in range(SUB):
            xb_ref[:, s * LANES:(s + 1) * LANES] = xs_ref[pl.ds(s, tile, stride=SUB), :].astype(BF16)

    @pl.when(jnp.logical_and(jnp.logical_not(valid), e == 0))
    def _():
        ys_ref[...] = jnp.zeros_like(ys_ref)

    @pl.when(valid)
    def _():
        x = xb_ref[...]
        hg = jnp.dot(x, wg_ref[0], preferred_element_type=F32)
        hu = jnp.dot(x, wu_ref[0], preferred_element_type=F32)
        route = rs_ref[...]
        lane = lax.broadcasted_iota(I32, route.shape, 1)
        cw = jnp.sum(jnp.where(lane == e, route, 0.0), axis=-1, keepdims=True)
        hmid = (hg * jax.nn.sigmoid(hg)) * hu * cw
        y = jnp.dot(hmid.astype(BF16), wd_ref[0], preferred_element_type=F32)

        @pl.when(e == 0)
        def _():
            acc_ref[...] = y

        @pl.when(jnp.logical_and(e > 0, e < last_e))
        def _():
            acc_ref[...] += y

        @pl.when(e == last_e)
        def _():
            _store_token_major(ys_ref, acc_ref[...] + y, tile)


def _moe(meta, xs, rs, wg, wu, wd, tile):
    n_rows = rs.shape[0]
    nt = n_rows // tile

    def row_map(i, e, meta_ref):
        last = jnp.maximum(meta_ref[META_TOTAL] // tile - 1, 0)
        return (jnp.minimum(i, last), 0)

    def w_map(i, e, meta_ref):
        last = jnp.maximum(meta_ref[META_TOTAL] // tile - 1, 0)
        g = _tile_group(meta_ref, jnp.minimum(i, last), tile)
        return (g * EXPERTS_PER_GROUP + e, 0, 0)

    return pl.pallas_call(
        functools.partial(_moe_body, tile=tile),
        grid_spec=pltpu.PrefetchScalarGridSpec(
            num_scalar_prefetch=1,
            grid=(nt, EXPERTS_PER_GROUP),
            in_specs=[
                pl.BlockSpec((tile * SUB, LANES), row_map),
                pl.BlockSpec((tile, LANES), row_map),
                pl.BlockSpec((1, D_MODEL, EXPERT_FF), w_map),
                pl.BlockSpec((1, D_MODEL, EXPERT_FF), w_map),
                pl.BlockSpec((1, EXPERT_FF, D_MODEL), w_map),
            ],
            out_specs=pl.BlockSpec((tile * SUB, LANES), lambda i, e, meta_ref: (i, 0)),
            scratch_shapes=[pltpu.VMEM((tile, D_MODEL), BF16), pltpu.VMEM((tile, D_MODEL), F32)],
        ),
        out_shape=jax.ShapeDtypeStruct((n_rows * SUB, LANES), F32),
        compiler_params=_cparams(("arbitrary", "arbitrary")),
        name="moe",
    )(meta, xs, rs, wg, wu, wd)


def _ln2_body(pos_ref, h_ref, ys_ref, p_ref, wp_ref, wpg_ref, g_ref, b_ref, o_ref, fbuf, sem, *, tm):
    base = pl.program_id(0) * tm

    def copy(t):
        return pltpu.make_async_copy(_token_block(ys_ref, pos_ref[base + t]), _token_block(fbuf, t), sem)

    def start(t, c):
        copy(t).start()
        return c

    def wait(t, c):
        copy(t).wait()
        return c

    lax.fori_loop(0, tm, start, 0)
    h1 = h_ref[...]
    ple = jnp.dot(p_ref[...].astype(BF16), wp_ref[...], preferred_element_type=F32)
    gate = jnp.dot(h1.astype(BF16), wpg_ref[...], preferred_element_type=F32)
    z = DN_ALPHA * h1 + ple * jax.nn.sigmoid(gate)
    lax.fori_loop(0, tm, wait, 0)
    ff = jnp.concatenate([fbuf[pl.ds(s, tm, stride=SUB), :] for s in range(SUB)], axis=1)
    o_ref[...] = _layer_norm(z + ff, g_ref[...], b_ref[...])


def _ln2(pos, h1, ys, p, wple, wpg, g, b, tm):
    T = h1.shape[0]
    return pl.pallas_call(
        functools.partial(_ln2_body, tm=tm),
        grid_spec=pltpu.PrefetchScalarGridSpec(
            num_scalar_prefetch=1,
            grid=(T // tm,),
            in_specs=[
                pl.BlockSpec((tm, D_MODEL), lambda i, pos_ref: (i, 0)),
                pl.BlockSpec(memory_space=pl.ANY),
                pl.BlockSpec((tm, PLE_DIM), lambda i, pos_ref: (i, 0)),
                pl.BlockSpec((PLE_DIM, D_MODEL), lambda i, pos_ref: (0, 0)),
                pl.BlockSpec((D_MODEL, D_MODEL), lambda i, pos_ref: (0, 0)),
                pl.BlockSpec((1, D_MODEL), lambda i, pos_ref: (0, 0)),
                pl.BlockSpec((1, D_MODEL), lambda i, pos_ref: (0, 0)),
            ],
            out_specs=pl.BlockSpec((tm, D_MODEL), lambda i, pos_ref: (i, 0)),
            scratch_shapes=[pltpu.VMEM((tm * SUB, LANES), F32), pltpu.SemaphoreType.DMA(())],
        ),
        out_shape=jax.ShapeDtypeStruct((T, D_MODEL), F32),
        compiler_params=_cparams(("arbitrary",)),
        name="ple_ln2",
    )(pos, h1, ys, p, wple, wpg, g, b)


def _pack_in_weights(w_in_l, m_i_bias_l, m_f_bias_l):
    parts, off = {}, 0
    for name, n in _SPLITS:
        parts[name] = w_in_l[:, off:off + n]
        off += n
    z64 = jnp.zeros((D_MODEL, IDX_DIM), w_in_l.dtype)
    wp = jnp.concatenate(
        [parts["a_q"], parts["m_q"], parts["m_k"], parts["m_v"], parts["m_o"], parts["g_a"], parts["g_m"],
         parts["i_q"], parts["a_ckv"], parts["i_k"], z64, z64, parts["i_k"]], axis=1).astype(BF16)
    pad = jnp.zeros((D_MODEL, LANES - IDX_HEADS - 2 * M_HEADS), w_in_l.dtype)
    ws = jnp.concatenate([parts["i_w"], parts["m_i"], parts["m_f"], pad], axis=1).astype(BF16)
    bias = jnp.concatenate([jnp.zeros((IDX_HEADS,), F32), m_i_bias_l.astype(F32), m_f_bias_l.astype(F32),
                            jnp.zeros((LANES - IDX_HEADS - 2 * M_HEADS,), F32)])
    return wp, ws, ws.T, bias.reshape(1, LANES), bias.reshape(LANES, 1)


def _tiles(T, S):
    return dict(
        inproj_tm=min(1024, T), inproj_tn=1536, row_tm=min(512, T), mix_tn=1024,
        attn_tq=min(256, S), mlstm_l=min(256, S), moe_tile=min(512, T // 4), dma_rows=min(256, T),
    )


def kernel(x, p, w_in, kv_norm, w_ukv, m_i_bias, m_f_bias, w_proj_a, w_proj_m, w_out, ln1_g, ln1_b, w_grp, b_grp,
           w_exp, b_exp, w_gate, w_up, w_down, w_ple, w_ple_gate, ln2_g, ln2_b):
    B, S, D = x.shape
    T = B * S
    assert D == D_MODEL and T % LANES == 0
    cfg = _tiles(T, S)
    h = x.reshape(T, D)
    for l in range(w_in.shape[0]):
        wp, ws, wst, brow, bcol = _pack_in_weights(w_in[l], m_i_bias[l], m_f_bias[l])
        proj, g, gt = _inproj(h, wp, ws, wst, brow, bcol, cfg["inproj_tm"], cfg["inproj_tn"])
        kv = _kv(proj, kv_norm[l].reshape(1, -1).astype(F32), w_ukv[l].astype(BF16), cfg["row_tm"])
        y_a = _attn(proj, kv, g, B, S, cfg["attn_tq"])
        y_m = _mlstm(proj, gt, B, S, cfg["mlstm_l"])
        mix = _mix(y_a, y_m, w_proj_a[l].astype(BF16), w_proj_m[l].astype(BF16), proj, cfg["row_tm"], cfg["mix_tn"])

        n_route = N_GROUPS + N_EXPERTS
        wr = jnp.concatenate([w_grp[l], w_exp[l], jnp.zeros((D, LANES - n_route), F32)], axis=1)
        br = jnp.concatenate([b_grp[l], b_exp[l], jnp.zeros((LANES - n_route,), F32)])
        h1, route, rt = _ln1(mix, h, w_out[l].astype(BF16), ln1_g[l].reshape(1, D), ln1_b[l].reshape(1, D),
                             wr.T.astype(BF16), br.reshape(LANES, 1), cfg["row_tm"])

        tile = cfg["moe_tile"]
        pos, meta = _positions(rt[N_GROUPS].reshape(T // LANES, LANES), tile)
        pos = pos.reshape(T)
        meta = meta.reshape(LANES)
        xs, rs = _scatter_rows(pos, meta, h1, route, T + N_GROUPS * tile, cfg["dma_rows"])
        ys = _moe(meta, xs, rs, w_gate[l].astype(BF16), w_up[l].astype(BF16), w_down[l].astype(BF16), tile)
        h = _ln2(pos, h1, ys, p[l].reshape(T, PLE_DIM), w_ple[l].astype(BF16), w_ple_gate[l].astype(BF16),
                 ln2_g[l].reshape(1, D), ln2_b[l].reshape(1, D), cfg["row_tm"])
    return h.reshape(B, S, D)
```

```python
import functools

import jax
import jax.numpy as jnp
from jax import lax
from jax.experimental import pallas as pl
from jax.experimental.pallas import tpu as pltpu

F32 = jnp.float32
BF16 = jnp.bfloat16
I32 = jnp.int32

D_MODEL = 2048
DEPTH = 2
A_HEADS = 16
A_HEAD_DIM = 128
A_KV_RANK = 256
IDX_HEADS = 16
IDX_DIM = 64
TOPK_MAX = 256
M_HEADS = 8
M_QK_DIM = 128
M_V_DIM = 256
N_GROUPS = 4
EXPERTS_PER_GROUP = 4
N_EXPERTS = N_GROUPS * EXPERTS_PER_GROUP
EXPERT_FF = 512
PLE_DIM = 256
LN_EPS = 1e-5
DN_ALPHA = (2 * DEPTH) ** 0.25

_SPLITS = (
    ("a_q", A_HEADS * A_HEAD_DIM), ("a_ckv", A_KV_RANK), ("i_q", IDX_HEADS * IDX_DIM), ("i_k", IDX_DIM),
    ("i_w", IDX_HEADS), ("m_q", M_HEADS * M_QK_DIM), ("m_k", M_HEADS * M_QK_DIM), ("m_v", M_HEADS * M_V_DIM),
    ("m_o", M_HEADS * M_V_DIM), ("m_i", M_HEADS), ("m_f", M_HEADS), ("g_a", D_MODEL), ("g_m", D_MODEL),
)

C_AQ, C_MQ, C_MK, C_MV, C_MO, C_GA, C_GM = 0, 2048, 3072, 4096, 6144, 8192, 10240
C_IQ, C_CKV, C_IKA, C_IKB = 12288, 13312, 13568, 13696
N_MAIN = 13824
G_IW, G_MI, G_MF = 0, 16, 24
LANES = 128

INT_MIN = -(2 ** 31)
NEG_BIG = -1e30
LOG2_E = 1.4426950408889634
VMEM_LIMIT = 56 * 1024 * 1024

_NT = (((1,), (1,)), ((), ()))
_TN = (((0,), (0,)), ((), ()))


def _cparams(sem):
    return pltpu.CompilerParams(dimension_semantics=sem, vmem_limit_bytes=VMEM_LIMIT)


def _inproj_body(x_ref, w_ref, ws_ref, wst_ref, brow_ref, bcol_ref, o_ref, g_ref, gt_ref, xb_ref):
    @pl.when(pl.program_id(1) == 0)
    def _():
        xb = x_ref[...].astype(BF16)
        xb_ref[...] = xb
        g_ref[...] = jnp.dot(xb, ws_ref[...], preferred_element_type=F32) + brow_ref[...]
        gt_ref[...] = lax.dot_general(wst_ref[...], xb, _NT, preferred_element_type=F32) + bcol_ref[...]

    o_ref[...] = jnp.dot(xb_ref[...], w_ref[...], preferred_element_type=F32).astype(BF16)


def _inproj(x, wp, ws, wst, brow, bcol, tm, tn):
    T = x.shape[0]
    return pl.pallas_call(
        _inproj_body,
        grid=(T // tm, N_MAIN // tn),
        in_specs=[
            pl.BlockSpec((tm, D_MODEL), lambda i, j: (i, 0)),
            pl.BlockSpec((D_MODEL, tn), lambda i, j: (0, j)),
            pl.BlockSpec((D_MODEL, LANES), lambda i, j: (0, 0)),
            pl.BlockSpec((LANES, D_MODEL), lambda i, j: (0, 0)),
            pl.BlockSpec((1, LANES), lambda i, j: (0, 0)),
            pl.BlockSpec((LANES, 1), lambda i, j: (0, 0)),
        ],
        out_specs=[
            pl.BlockSpec((tm, tn), lambda i, j: (i, j)),
            pl.BlockSpec((tm, LANES), lambda i, j: (i, 0)),
            pl.BlockSpec((LANES, tm), lambda i, j: (0, i)),
        ],
        out_shape=[
            jax.ShapeDtypeStruct((T, N_MAIN), BF16),
            jax.ShapeDtypeStruct((T, LANES), F32),
            jax.ShapeDtypeStruct((LANES, T), F32),
        ],
        scratch_shapes=[pltpu.VMEM((tm, D_MODEL), BF16)],
        compiler_params=_cparams(("arbitrary", "arbitrary")),
        name="inproj",
    )(x, wp, ws, wst, brow, bcol)


def _kv_body(c_ref, gn_ref, w_ref, o_ref):
    c = c_ref[...].astype(F32)
    r = c * lax.rsqrt(jnp.mean(c * c, axis=-1, keepdims=True) + LN_EPS) * gn_ref[...]
    o_ref[...] = jnp.dot(r.astype(BF16), w_ref[...], preferred_element_type=F32).astype(BF16)


def _kv(proj, gn, wukv, tm):
    T = proj.shape[0]
    return pl.pallas_call(
        _kv_body,
        grid=(T // tm,),
        in_specs=[
            pl.BlockSpec((tm, A_KV_RANK), lambda i: (i, C_CKV // A_KV_RANK)),
            pl.BlockSpec((1, A_KV_RANK), lambda i: (0, 0)),
            pl.BlockSpec((A_KV_RANK, 2 * A_HEAD_DIM), lambda i: (0, 0)),
        ],
        out_specs=pl.BlockSpec((tm, 2 * A_HEAD_DIM), lambda i: (i, 0)),
        out_shape=jax.ShapeDtypeStruct((T, 2 * A_HEAD_DIM), BF16),
        compiler_params=_cparams(("arbitrary",)),
        name="kv",
    )(proj, gn, wukv)


def _attn_body(q_ref, iq_ref, ika_ref, ikb_ref, kv_ref, g_ref, o_ref, key_ref, mb_ref, cand_ref, ans_ref, m_ref,
               acc_ref, *, tq, n_keep):
    qi = pl.program_id(1)
    nck = qi + 1
    t0 = qi * tq
    row_t = t0 + lax.broadcasted_iota(I32, (tq, tq), 0)
    col_l = lax.broadcasted_iota(I32, (tq, tq), 1)

    wq = g_ref[:, G_IW:G_IW + IDX_HEADS] * (IDX_HEADS ** -0.5 * IDX_DIM ** -0.5)

    def score_chunk(c, carry):
        k0 = pl.multiple_of(c * tq, tq)
        ka = ika_ref[pl.ds(k0, tq), :]
        kb = ikb_ref[pl.ds(k0, tq), :]
        acc = jnp.zeros((tq, tq), F32)
        for p in range(IDX_HEADS // 2):
            pair = iq_ref[:, p * LANES:(p + 1) * LANES]
            xe = lax.dot_general(pair, ka, _NT, preferred_element_type=F32)
            xo = lax.dot_general(pair, kb, _NT, preferred_element_type=F32)
            acc = acc + wq[:, 2 * p:2 * p + 1] * jnp.maximum(xe, 0.0)
            acc = acc + wq[:, 2 * p + 1:2 * p + 2] * jnp.maximum(xo, 0.0)
        bits = lax.bitcast_convert_type(acc, I32)
        key = bits ^ ((bits >> 31) & 0x7FFFFFFF)
        key_ref[c] = jnp.where(k0 + col_l <= row_t, key, INT_MIN)
        return carry

    lax.fori_loop(0, nck, score_chunk, 0)

    rows = min(tq, LANES)

    ones_sq = jnp.ones((LANES, LANES), BF16)

    def try_candidate():
        cnts = []
        for r in range(tq // rows):
            rs = slice(r * rows, (r + 1) * rows)

            def count_chunk(c, cnt, rs=rs):
                cb = cand_ref[rs, :]
                k = key_ref[c, rs, :]
                for v in range(tq // LANES):
                    cnt = cnt + jnp.where(k[:, v * LANES:(v + 1) * LANES] >= cb, 1.0, 0.0)
                return cnt

            cnts.append(lax.fori_loop(0, nck, count_chunk, jnp.zeros((rows, LANES), F32)))
        tot = jnp.dot(jnp.concatenate(cnts, axis=0).astype(BF16), ones_sq, preferred_element_type=F32)
        ans_ref[...] = jnp.where(tot >= n_keep, cand_ref[...], ans_ref[...])

    ans_ref[...] = jnp.full((tq, LANES), INT_MIN, I32)
    cand_ref[...] = jnp.zeros((tq, LANES), I32)
    try_candidate()

    def bisect(b, carry):
        cand_ref[...] = ans_ref[...] | lax.shift_left(jnp.int32(1), 30 - b)
        try_candidate()
        return carry

    lax.fori_loop(0, 31, bisect, 0)
    tau = jnp.where(row_t[:, 0:LANES] < n_keep, INT_MIN + 1, ans_ref[...])
    tau = jnp.concatenate([tau] * (tq // LANES), axis=1)

    def mask_chunk(c, carry):
        mb_ref[c] = jnp.where(key_ref[c] >= tau, 0.0, NEG_BIG)
        return carry

    lax.fori_loop(0, nck, mask_chunk, 0)

    scale = A_HEAD_DIM ** -0.5
    c_exp = scale * LOG2_E
    m_ref[...] = jnp.full(m_ref.shape, NEG_BIG, F32)
    acc_ref[...] = jnp.zeros(acc_ref.shape, F32)
    ones = jnp.ones((tq, A_HEAD_DIM), BF16)

    def att_chunk(c, carry):
        k0 = pl.multiple_of(c * tq, tq)
        kc = kv_ref[pl.ds(k0, tq), 0:A_HEAD_DIM]
        v1 = jnp.concatenate([kv_ref[pl.ds(k0, tq), A_HEAD_DIM:2 * A_HEAD_DIM], ones], axis=1)
        rel = (k0 - t0 + lax.broadcasted_iota(I32, (1, tq), 1)).astype(F32)
        for h in range(A_HEADS):
            srow = (2.0 ** (-8.0 * (h + 1) / A_HEADS) / scale) * rel
            for r in range(tq // rows):
                rs = slice(r * rows, (r + 1) * rows)
                qh = q_ref[rs, h * A_HEAD_DIM:(h + 1) * A_HEAD_DIM]
                u = lax.dot_general(qh, kc, _NT, preferred_element_type=F32) + mb_ref[c, rs, :] + srow
                um = u[:, 0:LANES]
                for v in range(1, tq // LANES):
                    um = jnp.maximum(um, u[:, v * LANES:(v + 1) * LANES])
                m = m_ref[h, rs, :]
                m_new = jnp.maximum(m, jnp.broadcast_to(jnp.max(um, axis=-1, keepdims=True), (rows, LANES)))
                m_ref[h, rs, :] = m_new
                alpha = jnp.exp2((m - m_new) * c_exp)
                p = jnp.concatenate([jnp.exp2((u[:, v * LANES:(v + 1) * LANES] - m_new) * c_exp)
                                     for v in range(tq // LANES)], axis=1)
                pv = jnp.dot(p.astype(BF16), v1, preferred_element_type=F32)
                acc = acc_ref[h, rs, :]
                acc_ref[h, rs, :] = jnp.concatenate(
                    [alpha * acc[:, v * LANES:(v + 1) * LANES] for v in range(2 * A_HEAD_DIM // LANES)], axis=1) + pv
        return carry

    lax.fori_loop(0, nck, att_chunk, 0)
    for h in range(A_HEADS):
        num = acc_ref[h, :, 0:A_HEAD_DIM]
        den = acc_ref[h, :, A_HEAD_DIM:2 * A_HEAD_DIM]
        o_ref[:, h * A_HEAD_DIM:(h + 1) * A_HEAD_DIM] = (num / den).astype(BF16)


def _attn(proj, kv, g, B, S, tq):
    T = proj.shape[0]
    nq = S // tq
    n_keep = min(TOPK_MAX, S // 4)
    qw = A_HEADS * A_HEAD_DIM
    iw = IDX_HEADS * IDX_DIM
    return pl.pallas_call(
        functools.partial(_attn_body, tq=tq, n_keep=n_keep),
        grid=(B, nq),
        in_specs=[
            pl.BlockSpec((tq, qw), lambda b, i: (b * nq + i, C_AQ // qw)),
            pl.BlockSpec((tq, iw), lambda b, i: (b * nq + i, C_IQ // iw)),
            pl.BlockSpec((S, LANES), lambda b, i: (b, C_IKA // LANES)),
            pl.BlockSpec((S, LANES), lambda b, i: (b, C_IKB // LANES)),
            pl.BlockSpec((S, 2 * A_HEAD_DIM), lambda b, i: (b, 0)),
            pl.BlockSpec((tq, LANES), lambda b, i: (b * nq + i, 0)),
        ],
        out_specs=pl.BlockSpec((tq, qw), lambda b, i: (b * nq + i, 0)),
        out_shape=jax.ShapeDtypeStruct((T, qw), BF16),
        scratch_shapes=[
            pltpu.VMEM((nq, tq, tq), I32),
            pltpu.VMEM((nq, tq, tq), F32),
            pltpu.VMEM((tq, LANES), I32),
            pltpu.VMEM((tq, LANES), I32),
            pltpu.VMEM((A_HEADS, tq, LANES), F32),
            pltpu.VMEM((A_HEADS, tq, 2 * A_HEAD_DIM), F32),
        ],
        compiler_params=_cparams(("arbitrary", "arbitrary")),
        name="dsa_attn",
    )(proj, proj, proj, proj, kv, g)


def _mlstm_body(q_ref, k_ref, v_ref, o_ref, gt_ref, y_ref, c_ref, n_ref, m_ref, *, L):
    H = M_HEADS

    @pl.when(pl.program_id(1) == 0)
    def _():
        c_ref[...] = jnp.zeros_like(c_ref)
        n_ref[...] = jnp.zeros_like(n_ref)
        m_ref[...] = jnp.zeros_like(m_ref)

    li = gt_ref[G_MI:G_MI + H, :]
    gf = gt_ref[G_MF:G_MF + H, :]
    lf = jnp.minimum(gf, 0.0) - jnp.log1p(jnp.exp(-jnp.abs(gf)))
    lane = lax.broadcasted_iota(I32, (H, L), 1)
    bcum = lf
    sh = 1
    while sh < L:
        bcum = bcum + jnp.where(lane >= sh, pltpu.roll(bcum, sh, axis=1), 0.0)
        sh *= 2
    b_last = bcum[:, L - 1:L]
    w_end = b_last - bcum + li
    a_end = jnp.max(w_end, axis=-1, keepdims=True)
    wexp = jnp.exp(w_end - a_end)
    m_old = m_ref[:, 0:1]
    m_new = jnp.maximum(b_last + m_old, a_end)
    decay = jnp.exp(b_last + m_old - m_new)
    scl = jnp.exp(a_end - m_new)
    inter_log = bcum + m_old
    crow = li - bcum
    m_ref[...] = jnp.broadcast_to(m_new, m_ref.shape)

    stack = jnp.concatenate([bcum, wexp, inter_log, jnp.zeros((LANES - 3 * H, L), F32)], axis=0)
    cols = stack.T

    tri = lax.broadcasted_iota(I32, (L, L), 1) <= lax.broadcasted_iota(I32, (L, L), 0)
    sc = M_QK_DIM ** -0.5
    for h in range(H):
        qh = q_ref[:, h * M_QK_DIM:(h + 1) * M_QK_DIM]
        kh = k_ref[:, h * M_QK_DIM:(h + 1) * M_QK_DIM]
        vh = v_ref[:, h * M_V_DIM:(h + 1) * M_V_DIM]
        oh = o_ref[:, h * M_V_DIM:(h + 1) * M_V_DIM]
        bcum_c = cols[:, h:h + 1]
        wexp_c = cols[:, H + h:H + h + 1]
        ilog_c = cols[:, 2 * H + h:2 * H + h + 1]
        C = c_ref[h]
        nrow = n_ref[h:h + 1, :]

        inter_num = jnp.dot(qh, C.astype(BF16), preferred_element_type=F32) * sc
        inter_den = jnp.sum(qh.astype(F32) * nrow, axis=-1, keepdims=True) * sc
        qk = lax.dot_general(qh, kh, _NT, preferred_element_type=F32) * sc
        dmat = jnp.where(tri, bcum_c + crow[h:h + 1, :], -jnp.inf)
        m_row = jnp.maximum(ilog_c, jnp.max(dmat, axis=-1, keepdims=True))
        inter_w = jnp.exp(ilog_c - m_row)
        intra_w = jnp.exp(dmat - m_row) * qk
        num = inter_w * inter_num + jnp.dot(intra_w.astype(BF16), vh, preferred_element_type=F32)
        den = inter_w * inter_den + jnp.sum(intra_w, axis=-1, keepdims=True)
        hout = num / jnp.maximum(jnp.abs(den), jnp.exp(-m_row))
        y_ref[:, h * M_V_DIM:(h + 1) * M_V_DIM] = (hout * jax.nn.sigmoid(oh.astype(F32))).astype(BF16)

        kw = kh.astype(F32) * wexp_c
        d_h = decay[h:h + 1, :]
        s_h = scl[h:h + 1, :]
        upd = lax.dot_general(kw.astype(BF16), vh, _TN, preferred_element_type=F32)
        c_ref[h] = d_h * C + s_h * upd
        n_ref[h:h + 1, :] = d_h * nrow + s_h * jnp.sum(kw, axis=0, keepdims=True)


def _mlstm(proj, gt, B, S, L):
    T = proj.shape[0]
    nc = S // L
    qw = M_HEADS * M_QK_DIM
    vw = M_HEADS * M_V_DIM
    return pl.pallas_call(
        functools.partial(_mlstm_body, L=L),
        grid=(B, nc),
        in_specs=[
            pl.BlockSpec((L, qw), lambda b, c: (b * nc + c, C_MQ // qw)),
            pl.BlockSpec((L, qw), lambda b, c: (b * nc + c, C_MK // qw)),
            pl.BlockSpec((L, vw), lambda b, c: (b * nc + c, C_MV // vw)),
            pl.BlockSpec((L, vw), lambda b, c: (b * nc + c, C_MO // vw)),
            pl.BlockSpec((LANES, L), lambda b, c: (0, b * nc + c)),
        ],
        out_specs=pl.BlockSpec((L, vw), lambda b, c: (b * nc + c, 0)),
        out_shape=jax.ShapeDtypeStruct((T, vw), BF16),
        scratch_shapes=[
            pltpu.VMEM((M_HEADS, M_QK_DIM, M_V_DIM), F32),
            pltpu.VMEM((M_HEADS, M_QK_DIM), F32),
            pltpu.VMEM((M_HEADS, LANES), F32),
        ],
        compiler_params=_cparams(("arbitrary", "arbitrary")),
        name="mlstm",
    )(proj, proj, proj, proj, gt)


def _mix_body(ya_ref, ym_ref, wa_ref, wm_ref, ga_ref, gm_ref, o_ref):
    a = jnp.dot(ya_ref[...], wa_ref[...], preferred_element_type=F32)
    m = jnp.dot(ym_ref[...], wm_ref[...], preferred_element_type=F32)
    mix = jax.nn.sigmoid(ga_ref[...].astype(F32)) * a + jax.nn.sigmoid(gm_ref[...].astype(F32)) * m
    o_ref[...] = mix.astype(BF16)


def _mix(ya, ym, wa, wm, proj, tm, tn):
    T = ya.shape[0]
    return pl.pallas_call(
        _mix_body,
        grid=(D_MODEL // tn, T // tm),
        in_specs=[
            pl.BlockSpec((tm, ya.shape[1]), lambda j, i: (i, 0)),
            pl.BlockSpec((tm, ym.shape[1]), lambda j, i: (i, 0)),
            pl.BlockSpec((wa.shape[0], tn), lambda j, i: (0, j)),
            pl.BlockSpec((wm.shape[0], tn), lambda j, i: (0, j)),
            pl.BlockSpec((tm, tn), lambda j, i: (i, C_GA // tn + j)),
            pl.BlockSpec((tm, tn), lambda j, i: (i, C_GM // tn + j)),
        ],
        out_specs=pl.BlockSpec((tm, tn), lambda j, i: (i, j)),
        out_shape=jax.ShapeDtypeStruct((T, D_MODEL), BF16),
        compiler_params=_cparams(("arbitrary", "arbitrary")),
        name="mix",
    )(ya, ym, wa, wm, proj, proj)


def _layer_norm(z, g, b):
    mu = jnp.mean(z, axis=-1, keepdims=True)
    d = z - mu
    var = jnp.mean(d * d, axis=-1, keepdims=True)
    return d * lax.rsqrt(var + LN_EPS) * g + b


def _first_max_index(vals):
    vmax = vals[0]
    for v in vals[1:]:
        vmax = jnp.maximum(vmax, v)
    idx = jnp.full(vmax.shape, len(vals) - 1, I32)
    for k in range(len(vals) - 2, -1, -1):
        idx = jnp.where(vals[k] == vmax, k, idx)
    return idx, vmax


def _ln1_body(mix_ref, h_ref, wo_ref, g_ref, b_ref, wrt_ref, brt_ref, h1_ref, route_ref, rt_ref, *, tm):
    z = DN_ALPHA * h_ref[...] + jnp.dot(mix_ref[...], wo_ref[...], preferred_element_type=F32)
    h1 = _layer_norm(z, g_ref[...], b_ref[...])
    h1_ref[...] = h1

    r = lax.dot_general(wrt_ref[...], h1.astype(BF16), _NT, preferred_element_type=F32) + brt_ref[...]
    gl = [r[k:k + 1, :] for k in range(N_GROUPS)]
    g_idx, g_max = _first_max_index(gl)
    g_sum = gl[0] * 0.0
    for v in gl:
        g_sum = g_sum + jnp.exp(v - g_max)
    g_p = 1.0 / g_sum
    el = []
    for e in range(EXPERTS_PER_GROUP):
        rows_e = [N_GROUPS + g * EXPERTS_PER_GROUP + e for g in range(N_GROUPS)]
        v = r[rows_e[-1]:rows_e[-1] + 1, :]
        for g in range(N_GROUPS - 2, -1, -1):
            v = jnp.where(g_idx == g, r[rows_e[g]:rows_e[g] + 1, :], v)
        el.append(v)
    _, e_max = _first_max_index(el)
    ex = [jnp.exp(v - e_max) for v in el]
    e_sum = ex[0] + ex[1] + ex[2] + ex[3]
    ep = [v / e_sum for v in ex]
    i1, p1 = _first_max_index(ep)
    rest = [jnp.where(i1 == e, -1.0, ep[e]) for e in range(EXPERTS_PER_GROUP)]
    i2, p2 = _first_max_index(rest)
    den = p1 + p2
    rows = []
    for e in range(EXPERTS_PER_GROUP):
        w = jnp.where(i1 == e, p1 / den, jnp.where(i2 == e, p2 / den, 0.0))
        rows.append(g_p * w)
    rows.append(g_idx.astype(F32))
    rows.append(jnp.zeros((8 - len(rows), tm), F32))
    route_t = jnp.concatenate(rows, axis=0)
    rt_ref[...] = route_t
    full = jnp.concatenate([route_t, jnp.zeros((LANES - 8, tm), F32)], axis=0)
    route_ref[...] = full.T


def _ln1(mix, h, wo, g, b, wrt, brt, tm):
    T = mix.shape[0]
    return pl.pallas_call(
        functools.partial(_ln1_body, tm=tm),
        grid=(T // tm,),
        in_specs=[
            pl.BlockSpec((tm, D_MODEL), lambda i: (i, 0)),
            pl.BlockSpec((tm, D_MODEL), lambda i: (i, 0)),
            pl.BlockSpec((D_MODEL, D_MODEL), lambda i: (0, 0)),
            pl.BlockSpec((1, D_MODEL), lambda i: (0, 0)),
            pl.BlockSpec((1, D_MODEL), lambda i: (0, 0)),
            pl.BlockSpec((LANES, D_MODEL), lambda i: (0, 0)),
            pl.BlockSpec((LANES, 1), lambda i: (0, 0)),
        ],
        out_specs=[
            pl.BlockSpec((tm, D_MODEL), lambda i: (i, 0)),
            pl.BlockSpec((tm, LANES), lambda i: (i, 0)),
            pl.BlockSpec((8, tm), lambda i: (0, i)),
        ],
        out_shape=[
            jax.ShapeDtypeStruct((T, D_MODEL), F32),
            jax.ShapeDtypeStruct((T, LANES), F32),
            jax.ShapeDtypeStruct((8, T), F32),
        ],
        compiler_params=_cparams(("arbitrary",)),
        name="ln1_router",
    )(mix, h, wo, g, b, wrt, brt)


META_START, META_COUNT, META_TOTAL = 0, N_GROUPS, 2 * N_GROUPS


def _pos_body(gid_ref, pos_ref, meta_ref, *, tile, R):
    gid = gid_ref[...]
    upper = (lax.broadcasted_iota(I32, (LANES, LANES), 0) <= lax.broadcasted_iota(I32, (LANES, LANES), 1))
    upper = jnp.where(upper, 1.0, 0.0).astype(BF16)
    below = lax.broadcasted_iota(I32, (R, R), 1) < lax.broadcasted_iota(I32, (R, R), 0)
    below = jnp.where(below, 1.0, 0.0).astype(BF16)
    lane = lax.broadcasted_iota(I32, (1, LANES), 1)
    pos = jnp.zeros((R, LANES), F32)
    meta = jnp.zeros((1, LANES), F32)
    start = jnp.zeros((1, 1), F32)
    for g in range(N_GROUPS):
        oh = gid == float(g)
        incl = jnp.dot(jnp.where(oh, 1.0, 0.0).astype(BF16), upper, preferred_element_type=F32)
        rowtot = incl[:, LANES - 1:LANES]
        before = jnp.dot(below, jnp.broadcast_to(rowtot, (R, LANES)).astype(BF16), preferred_element_type=F32)
        cnt = jnp.sum(rowtot, axis=0, keepdims=True)
        pos = jnp.where(oh, start + before + incl - 1.0, pos)
        meta = jnp.where(lane == META_START + g, start, meta)
        meta = jnp.where(lane == META_COUNT + g, cnt, meta)
        start = start + jnp.floor((cnt + (tile - 1)) * (1.0 / tile)) * tile
    meta = jnp.where(lane == META_TOTAL, start, meta)
    pos_ref[...] = pos.astype(I32)
    meta_ref[...] = meta.astype(I32)


def _positions(gid, tile):
    R = gid.shape[0]
    return pl.pallas_call(
        functools.partial(_pos_body, tile=tile, R=R),
        out_shape=[jax.ShapeDtypeStruct((R, LANES), I32), jax.ShapeDtypeStruct((1, LANES), I32)],
        name="positions",
    )(gid)


SUB = D_MODEL // LANES


def _store_token_major(dst_ref, val, n):
    for s in range(SUB):
        dst_ref[pl.ds(s, n, stride=SUB), :] = val[:, s * LANES:(s + 1) * LANES]


def _token_block(ref, t):
    return ref.at[pl.ds(pl.multiple_of(t * SUB, SUB), SUB), :]


def _scatter_body(pos_ref, meta_ref, h_ref, route_ref, xs_ref, rs_ref, xbuf, zx_ref, zr_ref, xsem, rsem, *, nb):
    i = pl.program_id(0)
    base = i * nb

    @pl.when(i == 0)
    def _():
        zx_ref[...] = jnp.zeros_like(zx_ref)
        zr_ref[...] = jnp.zeros_like(zr_ref)
        for g in range(N_GROUPS + 1):
            if g < N_GROUPS:
                lo = meta_ref[META_START + g] + meta_ref[META_COUNT + g]
                hi = meta_ref[META_START + g + 1] if g + 1 < N_GROUPS else meta_ref[META_TOTAL]
            else:
                lo, hi = meta_ref[META_TOTAL], rs_ref.shape[0]

            def zcopies(r):
                return (pltpu.make_async_copy(zx_ref, _token_block(xs_ref, r), xsem),
                        pltpu.make_async_copy(zr_ref, rs_ref.at[pl.ds(r, 1), :], rsem))

            def zstart(r, c):
                for cp in zcopies(r):
                    cp.start()
                return c

            def zwait(r, c):
                for cp in zcopies(r):
                    cp.wait()
                return c

            lax.fori_loop(lo, hi, zstart, 0)
            lax.fori_loop(lo, hi, zwait, 0)

    _store_token_major(xbuf, h_ref[...], nb)

    def copies(t):
        dst = pos_ref[base + t]
        return (pltpu.make_async_copy(_token_block(xbuf, t), _token_block(xs_ref, dst), xsem),
                pltpu.make_async_copy(route_ref.at[pl.ds(t, 1), :], rs_ref.at[pl.ds(dst, 1), :], rsem))

    def start(t, c):
        for cp in copies(t):
            cp.start()
        return c

    def wait(t, c):
        for cp in copies(t):
            cp.wait()
        return c

    lax.fori_loop(0, nb, start, 0)
    lax.fori_loop(0, nb, wait, 0)


def _scatter_rows(pos, meta, h1, route, n_out, nb):
    T = h1.shape[0]
    return pl.pallas_call(
        functools.partial(_scatter_body, nb=nb),
        grid_spec=pltpu.PrefetchScalarGridSpec(
            num_scalar_prefetch=2,
            grid=(T // nb,),
            in_specs=[
                pl.BlockSpec((nb, D_MODEL), lambda i, pos_ref, meta_ref: (i, 0)),
                pl.BlockSpec((nb, LANES), lambda i, pos_ref, meta_ref: (i, 0)),
            ],
            out_specs=[pl.BlockSpec(memory_space=pl.ANY), pl.BlockSpec(memory_space=pl.ANY)],
            scratch_shapes=[
                pltpu.VMEM((nb * SUB, LANES), F32),
                pltpu.VMEM((SUB, LANES), F32),
                pltpu.VMEM((1, LANES), F32),
                pltpu.SemaphoreType.DMA(()),
                pltpu.SemaphoreType.DMA(()),
            ],
        ),
        out_shape=[jax.ShapeDtypeStruct((n_out * SUB, LANES), F32), jax.ShapeDtypeStruct((n_out, LANES), F32)],
        compiler_params=pltpu.CompilerParams(dimension_semantics=("arbitrary",), has_side_effects=True,
                                             vmem_limit_bytes=VMEM_LIMIT),
        name="scatter_rows",
    )(pos, meta, h1, route)


def _tile_group(meta_ref, i, tile):
    row = i * tile
    g = jnp.int32(0)
    for k in range(1, N_GROUPS):
        g = g + (row >= meta_ref[META_START + k]).astype(I32)
    return g


def _moe_body(meta_ref, xs_ref, rs_ref, wg_ref, wu_ref, wd_ref, ys_ref, xb_ref, acc_ref, *, tile):
    i = pl.program_id(0)
    e = pl.program_id(1)
    last_e = EXPERTS_PER_GROUP - 1
    valid = i * tile < meta_ref[META_TOTAL]

    @pl.when(jnp.logical_and(valid, e == 0))
    def _():
        for s in range(SUB):
            xb_ref[:, s * LANES:(s + 1) * LANES] = xs_ref[pl.ds(s, tile, stride=SUB), :].astype(BF16)

    @pl.when(jnp.logical_and(jnp.logical_not(valid), e == 0))
    def _():
        ys_ref[...] = jnp.zeros_like(ys_ref)

    @pl.when(valid)
    def _():
        x = xb_ref[...]
        hg = jnp.dot(x, wg_ref[0], preferred_element_type=F32)
        hu = jnp.dot(x, wu_ref[0], preferred_element_type=F32)
        route = rs_ref[...]
        lane = lax.broadcasted_iota(I32, route.shape, 1)
        cw = jnp.sum(jnp.where(lane == e, route, 0.0), axis=-1, keepdims=True)
        hmid = (hg * jax.nn.sigmoid(hg)) * hu * cw
        y = jnp.dot(hmid.astype(BF16), wd_ref[0], preferred_element_type=F32)

        @pl.when(e == 0)
        def _():
            acc_ref[...] = y

        @pl.when(jnp.logical_and(e > 0, e < last_e))
        def _():
            acc_ref[...] += y

        @pl.when(e == last_e)
        def _():
            _store_token_major(ys_ref, acc_ref[...] + y, tile)


def _moe(meta, xs, rs, wg, wu, wd, tile):
    n_rows = rs.shape[0]
    nt = n_rows // tile

    def row_map(i, e, meta_ref):
        last = jnp.maximum(meta_ref[META_TOTAL] // tile - 1, 0)
        return (jnp.minimum(i, last), 0)

    def w_map(i, e, meta_ref):
        last = jnp.maximum(meta_ref[META_TOTAL] // tile - 1, 0)
        g = _tile_group(meta_ref, jnp.minimum(i, last), tile)
        return (g * EXPERTS_PER_GROUP + e, 0, 0)

    return pl.pallas_call(
        functools.partial(_moe_body, tile=tile),
        grid_spec=pltpu.PrefetchScalarGridSpec(
            num_scalar_prefetch=1,
            grid=(nt, EXPERTS_PER_GROUP),
            in_specs=[
                pl.BlockSpec((tile * SUB, LANES), row_map),
                pl.BlockSpec((tile, LANES), row_map),
                pl.BlockSpec((1, D_MODEL, EXPERT_FF), w_map),
                pl.BlockSpec((1, D_MODEL, EXPERT_FF), w_map),
                pl.BlockSpec((1, EXPERT_FF, D_MODEL), w_map),
            ],
            out_specs=pl.BlockSpec((tile * SUB, LANES), lambda i, e, meta_ref: (i, 0)),
            scratch_shapes=[pltpu.VMEM((tile, D_MODEL), BF16), pltpu.VMEM((tile, D_MODEL), F32)],
        ),
        out_shape=jax.ShapeDtypeStruct((n_rows * SUB, LANES), F32),
        compiler_params=_cparams(("arbitrary", "arbitrary")),
        name="moe",
    )(meta, xs, rs, wg, wu, wd)


def _ln2_body(pos_ref, h_ref, ys_ref, p_ref, wp_ref, wpg_ref, g_ref, b_ref, o_ref, fbuf, sem, *, tm):
    base = pl.program_id(0) * tm

    def copy(t):
        return pltpu.make_async_copy(_token_block(ys_ref, pos_ref[base + t]), _token_block(fbuf, t), sem)

    def start(t, c):
        copy(t).start()
        return c

    def wait(t, c):
        copy(t).wait()
        return c

    lax.fori_loop(0, tm, start, 0)
    h1 = h_ref[...]
    ple = jnp.dot(p_ref[...].astype(BF16), wp_ref[...], preferred_element_type=F32)
    gate = jnp.dot(h1.astype(BF16), wpg_ref[...], preferred_element_type=F32)
    z = DN_ALPHA * h1 + ple * jax.nn.sigmoid(gate)
    lax.fori_loop(0, tm, wait, 0)
    ff = jnp.concatenate([fbuf[pl.ds(s, tm, stride=SUB), :] for s in range(SUB)], axis=1)
    o_ref[...] = _layer_norm(z + ff, g_ref[...], b_ref[...])


def _ln2(pos, h1, ys, p, wple, wpg, g, b, tm):
    T = h1.shape[0]
    return pl.pallas_call(
        functools.partial(_ln2_body, tm=tm),
        grid_spec=pltpu.PrefetchScalarGridSpec(
            num_scalar_prefetch=1,
            grid=(T // tm,),
            in_specs=[
                pl.BlockSpec((tm, D_MODEL), lambda i, pos_ref: (i, 0)),
                pl.BlockSpec(memory_space=pl.ANY),
                pl.BlockSpec((tm, PLE_DIM), lambda i, pos_ref: (i, 0)),
                pl.BlockSpec((PLE_DIM, D_MODEL), lambda i, pos_ref: (0, 0)),
                pl.BlockSpec((D_MODEL, D_MODEL), lambda i, pos_ref: (0, 0)),
                pl.BlockSpec((1, D_MODEL), lambda i, pos_ref: (0, 0)),
                pl.BlockSpec((1, D_MODEL), lambda i, pos_ref: (0, 0)),
            ],
            out_specs=pl.BlockSpec((tm, D_MODEL), lambda i, pos_ref: (i, 0)),
            scratch_shapes=[pltpu.VMEM((tm * SUB, LANES), F32), pltpu.SemaphoreType.DMA(())],
        ),
        out_shape=jax.ShapeDtypeStruct((T, D_MODEL), F32),
        compiler_params=_cparams(("arbitrary",)),
        name="ple_ln2",
    )(pos, h1, ys, p, wple, wpg, g, b)


def _pack_in_weights(w_in_l, m_i_bias_l, m_f_bias_l):
    parts, off = {}, 0
    for name, n in _SPLITS:
        parts[name] = w_in_l[:, off:off + n]
        off += n
    z64 = jnp.zeros((D_MODEL, IDX_DIM), w_in_l.dtype)
    wp = jnp.concatenate(
        [parts["a_q"], parts["m_q"], parts["m_k"], parts["m_v"], parts["m_o"], parts["g_a"], parts["g_m"],
         parts["i_q"], parts["a_ckv"], parts["i_k"], z64, z64, parts["i_k"]], axis=1).astype(BF16)
    pad = jnp.zeros((D_MODEL, LANES - IDX_HEADS - 2 * M_HEADS), w_in_l.dtype)
    ws = jnp.concatenate([parts["i_w"], parts["m_i"], parts["m_f"], pad], axis=1).astype(BF16)
    bias = jnp.concatenate([jnp.zeros((IDX_HEADS,), F32), m_i_bias_l.astype(F32), m_f_bias_l.astype(F32),
                            jnp.zeros((LANES - IDX_HEADS - 2 * M_HEADS,), F32)])
    return wp, ws, ws.T, bias.reshape(1, LANES), bias.reshape(LANES, 1)


def _tiles(T, S):
    return dict(
        inproj_tm=min(1024, T), inproj_tn=1536, row_tm=min(512, T), mix_tn=1024,
        attn_tq=min(256, S), mlstm_l=min(256, S), moe_tile=min(512, T // 4), dma_rows=min(256, T),
    )


def kernel(x, p, w_in, kv_norm, w_ukv, m_i_bias, m_f_bias, w_proj_a, w_proj_m, w_out, ln1_g, ln1_b, w_grp, b_grp,
           w_exp, b_exp, w_gate, w_up, w_down, w_ple, w_ple_gate, ln2_g, ln2_b):
    B, S, D = x.shape
    T = B * S
    assert D == D_MODEL and T % LANES == 0
    cfg = _tiles(T, S)
    h = x.reshape(T, D)
    for l in range(w_in.shape[0]):
        wp, ws, wst, brow, bcol = _pack_in_weights(w_in[l], m_i_bias[l], m_f_bias[l])
        proj, g, gt = _inproj(h, wp, ws, wst, brow, bcol, cfg["inproj_tm"], cfg["inproj_tn"])
        kv = _kv(proj, kv_norm[l].reshape(1, -1).astype(F32), w_ukv[l].astype(BF16), cfg["row_tm"])
        y_a = _attn(proj, kv, g, B, S, cfg["attn_tq"])
        y_m = _mlstm(proj, gt, B, S, cfg["mlstm_l"])
        mix = _mix(y_a, y_m, w_proj_a[l].astype(BF16), w_proj_m[l].astype(BF16), proj, cfg["row_tm"], cfg["mix_tn"])

        n_route = N_GROUPS + N_EXPERTS
        wr = jnp.concatenate([w_grp[l], w_exp[l], jnp.zeros((D, LANES - n_route), F32)], axis=1)
        br = jnp.concatenate([b_grp[l], b_exp[l], jnp.zeros((LANES - n_route,), F32)])
        h1, route, rt = _ln1(mix, h, w_out[l].astype(BF16), ln1_g[l].reshape(1, D), ln1_b[l].reshape(1, D),
                             wr.T.astype(BF16), br.reshape(LANES, 1), cfg["row_tm"])

        tile = cfg["moe_tile"]
        pos, meta = _positions(rt[N_GROUPS].reshape(T // LANES, LANES), tile)
        pos = pos.reshape(T)
        meta = meta.reshape(LANES)
        xs, rs = _scatter_rows(pos, meta, h1, route, T + N_GROUPS * tile, cfg["dma_rows"])
        ys = _moe(meta, xs, rs, w_gate[l].astype(BF16), w_up[l].astype(BF16), w_down[l].astype(BF16), tile)
        h = _ln2(pos, h1, ys, p[l].reshape(T, PLE_DIM), w_ple[l].astype(BF16), w_ple_gate[l].astype(BF16),
                 ln2_g[l].reshape(1, D), ln2_b[l].reshape(1, D), cfg["row_tm"])
    return h.reshape(B, S, D)
```

```python
import functools

import jax
import jax.numpy as jnp
from jax import lax
from jax.experimental import pallas as pl
from jax.experimental.pallas import tpu as pltpu

F32 = jnp.float32
BF16 = jnp.bfloat16
I32 = jnp.int32

D_MODEL = 2048
DEPTH = 2
A_HEADS = 16
A_HEAD_DIM = 128
A_KV_RANK = 256
IDX_HEADS = 16
IDX_DIM = 64
TOPK_MAX = 256
M_HEADS = 8
M_QK_DIM = 128
M_V_DIM = 256
N_GROUPS = 4
EXPERTS_PER_GROUP = 4
N_EXPERTS = N_GROUPS * EXPERTS_PER_GROUP
EXPERT_FF = 512
PLE_DIM = 256
LN_EPS = 1e-5
DN_ALPHA = (2 * DEPTH) ** 0.25

_SPLITS = (
    ("a_q", A_HEADS * A_HEAD_DIM), ("a_ckv", A_KV_RANK), ("i_q", IDX_HEADS * IDX_DIM), ("i_k", IDX_DIM),
    ("i_w", IDX_HEADS), ("m_q", M_HEADS * M_QK_DIM), ("m_k", M_HEADS * M_QK_DIM), ("m_v", M_HEADS * M_V_DIM),
    ("m_o", M_HEADS * M_V_DIM), ("m_i", M_HEADS), ("m_f", M_HEADS), ("g_a", D_MODEL), ("g_m", D_MODEL),
)

C_AQ, C_MQ, C_MK, C_MV, C_MO, C_GA, C_GM = 0, 2048, 3072, 4096, 6144, 8192, 10240
C_IQ, C_CKV, C_IKA, C_IKB = 12288, 13312, 13568, 13696
N_MAIN = 13824
G_IW, G_MI, G_MF = 0, 16, 24
LANES = 128

INT_MIN = -(2 ** 31)
NEG_BIG = -1e30
LOG2_E = 1.4426950408889634
VMEM_LIMIT = 56 * 1024 * 1024

_NT = (((1,), (1,)), ((), ()))
_TN = (((0,), (0,)), ((), ()))


def _cparams(sem):
    return pltpu.CompilerParams(dimension_semantics=sem, vmem_limit_bytes=VMEM_LIMIT)


def _inproj_body(x_ref, w_ref, ws_ref, wst_ref, brow_ref, bcol_ref, o_ref, g_ref, gt_ref, xb_ref):
    @pl.when(pl.program_id(1) == 0)
    def _():
        xb = x_ref[...].astype(BF16)
        xb_ref[...] = xb
        g_ref[...] = jnp.dot(xb, ws_ref[...], preferred_element_type=F32) + brow_ref[...]
        gt_ref[...] = lax.dot_general(wst_ref[...], xb, _NT, preferred_element_type=F32) + bcol_ref[...]

    o_ref[...] = jnp.dot(xb_ref[...], w_ref[...], preferred_element_type=F32).astype(BF16)


def _inproj(x, wp, ws, wst, brow, bcol, tm, tn):
    T = x.shape[0]
    return pl.pallas_call(
        _inproj_body,
        grid=(T // tm, N_MAIN // tn),
        in_specs=[
            pl.BlockSpec((tm, D_MODEL), lambda i, j: (i, 0)),
            pl.BlockSpec((D_MODEL, tn), lambda i, j: (0, j)),
            pl.BlockSpec((D_MODEL, LANES), lambda i, j: (0, 0)),
            pl.BlockSpec((LANES, D_MODEL), lambda i, j: (0, 0)),
            pl.BlockSpec((1, LANES), lambda i, j: (0, 0)),
            pl.BlockSpec((LANES, 1), lambda i, j: (0, 0)),
        ],
        out_specs=[
            pl.BlockSpec((tm, tn), lambda i, j: (i, j)),
            pl.BlockSpec((tm, LANES), lambda i, j: (i, 0)),
            pl.BlockSpec((LANES, tm), lambda i, j: (0, i)),
        ],
        out_shape=[
            jax.ShapeDtypeStruct((T, N_MAIN), BF16),
            jax.ShapeDtypeStruct((T, LANES), F32),
            jax.ShapeDtypeStruct((LANES, T), F32),
        ],
        scratch_shapes=[pltpu.VMEM((tm, D_MODEL), BF16)],
        compiler_params=_cparams(("arbitrary", "arbitrary")),
        name="inproj",
    )(x, wp, ws, wst, brow, bcol)


def _kv_body(c_ref, gn_ref, w_ref, o_ref):
    c = c_ref[...].astype(F32)
    r = c * lax.rsqrt(jnp.mean(c * c, axis=-1, keepdims=True) + LN_EPS) * gn_ref[...]
    o_ref[...] = jnp.dot(r.astype(BF16), w_ref[...], preferred_element_type=F32).astype(BF16)


def _kv(proj, gn, wukv, tm):
    T = proj.shape[0]
    return pl.pallas_call(
        _kv_body,
        grid=(T // tm,),
        in_specs=[
            pl.BlockSpec((tm, A_KV_RANK), lambda i: (i, C_CKV // A_KV_RANK)),
            pl.BlockSpec((1, A_KV_RANK), lambda i: (0, 0)),
            pl.BlockSpec((A_KV_RANK, 2 * A_HEAD_DIM), lambda i: (0, 0)),
        ],
        out_specs=pl.BlockSpec((tm, 2 * A_HEAD_DIM), lambda i: (i, 0)),
        out_shape=jax.ShapeDtypeStruct((T, 2 * A_HEAD_DIM), BF16),
        compiler_params=_cparams(("arbitrary",)),
        name="kv",
    )(proj, gn, wukv)


def _attn_body(q_ref, iq_ref, ika_ref, ikb_ref, kv_ref, g_ref, o_ref, key_ref, mb_ref, cand_ref, ans_ref, m_ref,
               acc_ref, *, tq, n_keep):
    qi = pl.program_id(1)
    nck = qi + 1
    t0 = qi * tq
    row_t = t0 + lax.broadcasted_iota(I32, (tq, tq), 0)
    col_l = lax.broadcasted_iota(I32, (tq, tq), 1)

    wq = g_ref[:, G_IW:G_IW + IDX_HEADS] * (IDX_HEADS ** -0.5 * IDX_DIM ** -0.5)

    def score_chunk(c, carry):
        k0 = pl.multiple_of(c * tq, tq)
        ka = ika_ref[pl.ds(k0, tq), :]
        kb = ikb_ref[pl.ds(k0, tq), :]
        acc = jnp.zeros((tq, tq), F32)
        for p in range(IDX_HEADS // 2):
            pair = iq_ref[:, p * LANES:(p + 1) * LANES]
            xe = lax.dot_general(pair, ka, _NT, preferred_element_type=F32)
            xo = lax.dot_general(pair, kb, _NT, preferred_element_type=F32)
            acc = acc + wq[:, 2 * p:2 * p + 1] * jnp.maximum(xe, 0.0)
            acc = acc + wq[:, 2 * p + 1:2 * p + 2] * jnp.maximum(xo, 0.0)
        bits = lax.bitcast_convert_type(acc, I32)
        key = bits ^ ((bits >> 31) & 0x7FFFFFFF)
        key_ref[c] = jnp.where(k0 + col_l <= row_t, key, INT_MIN)
        return carry

    lax.fori_loop(0, nck, score_chunk, 0)

    rows = min(tq, LANES)

    ones_sq = jnp.ones((LANES, LANES), BF16)

    def try_candidate():
        cnts = []
        for r in range(tq // rows):
            rs = slice(r * rows, (r + 1) * rows)

            def count_chunk(c, cnt, rs=rs):
                cb = cand_ref[rs, :]
                k = key_ref[c, rs, :]
                for v in range(tq // LANES):
                    cnt = cnt + jnp.where(k[:, v * LANES:(v + 1) * LANES] >= cb, 1.0, 0.0)
                return cnt

            cnts.append(lax.fori_loop(0, nck, count_chunk, jnp.zeros((rows, LANES), F32)))
        tot = jnp.dot(jnp.concatenate(cnts, axis=0).astype(BF16), ones_sq, preferred_element_type=F32)
        ans_ref[...] = jnp.where(tot >= n_keep, cand_ref[...], ans_ref[...])

    ans_ref[...] = jnp.full((tq, LANES), INT_MIN, I32)
    cand_ref[...] = jnp.zeros((tq, LANES), I32)
    try_candidate()

    def bisect(b, carry):
        cand_ref[...] = ans_ref[...] | lax.shift_left(jnp.int32(1), 30 - b)
        try_candidate()
        return carry

    lax.fori_loop(0, 31, bisect, 0)
    tau = jnp.where(row_t[:, 0:LANES] < n_keep, INT_MIN + 1, ans_ref[...])
    tau = jnp.concatenate([tau] * (tq // LANES), axis=1)

    def mask_chunk(c, carry):
        mb_ref[c] = jnp.where(key_ref[c] >= tau, 0.0, NEG_BIG)
        return carry

    lax.fori_loop(0, nck, mask_chunk, 0)

    scale = A_HEAD_DIM ** -0.5
    c_exp = scale * LOG2_E
    m_ref[...] = jnp.full(m_ref.shape, NEG_BIG, F32)
    acc_ref[...] = jnp.zeros(acc_ref.shape, F32)
    ones = jnp.ones((tq, A_HEAD_DIM), BF16)

    def att_chunk(c, carry):
        k0 = pl.multiple_of(c * tq, tq)
        kc = kv_ref[pl.ds(k0, tq), 0:A_HEAD_DIM]
        v1 = jnp.concatenate([kv_ref[pl.ds(k0, tq), A_HEAD_DIM:2 * A_HEAD_DIM], ones], axis=1)
        rel = (k0 - t0 + lax.broadcasted_iota(I32, (1, tq), 1)).astype(F32)
        for h in range(A_HEADS):
            srow = (2.0 ** (-8.0 * (h + 1) / A_HEADS) / scale) * rel
            for r in range(tq // rows):
                rs = slice(r * rows, (r + 1) * rows)
                qh = q_ref[rs, h * A_HEAD_DIM:(h + 1) * A_HEAD_DIM]
                u = lax.dot_general(qh, kc, _NT, preferred_element_type=F32) + mb_ref[c, rs, :] + srow
                um = u[:, 0:LANES]
                for v in range(1, tq // LANES):
                    um = jnp.maximum(um, u[:, v * LANES:(v + 1) * LANES])
                m = m_ref[h, rs, :]
                m_new = jnp.maximum(m, jnp.broadcast_to(jnp.max(um, axis=-1, keepdims=True), (rows, LANES)))
                m_ref[h, rs, :] = m_new
                alpha = jnp.exp2((m - m_new) * c_exp)
                p = jnp.concatenate([jnp.exp2((u[:, v * LANES:(v + 1) * LANES] - m_new) * c_exp)
                                     for v in range(tq // LANES)], axis=1)
                pv = jnp.dot(p.astype(BF16), v1, preferred_element_type=F32)
                acc = acc_ref[h, rs, :]
                acc_ref[h, rs, :] = jnp.concatenate(
                    [alpha * acc[:, v * LANES:(v + 1) * LANES] for v in range(2 * A_HEAD_DIM // LANES)], axis=1) + pv
        return carry

    lax.fori_loop(0, nck, att_chunk, 0)
    for h in range(A_HEADS):
        num = acc_ref[h, :, 0:A_HEAD_DIM]
        den = acc_ref[h, :, A_HEAD_DIM:2 * A_HEAD_DIM]
        o_ref[:, h * A_HEAD_DIM:(h + 1) * A_HEAD_DIM] = (num / den).astype(BF16)


def _attn(proj, kv, g, B, S, tq):
    T = proj.shape[0]
    nq = S // tq
    n_keep = min(TOPK_MAX, S // 4)
    qw = A_HEADS * A_HEAD_DIM
    iw = IDX_HEADS * IDX_DIM
    return pl.pallas_call(
        functools.partial(_attn_body, tq=tq, n_keep=n_keep),
        grid=(B, nq),
        in_specs=[
            pl.BlockSpec((tq, qw), lambda b, i: (b * nq + i, C_AQ // qw)),
            pl.BlockSpec((tq, iw), lambda b, i: (b * nq + i, C_IQ // iw)),
            pl.BlockSpec((S, LANES), lambda b, i: (b, C_IKA // LANES)),
            pl.BlockSpec((S, LANES), lambda b, i: (b, C_IKB // LANES)),
            pl.BlockSpec((S, 2 * A_HEAD_DIM), lambda b, i: (b, 0)),
            pl.BlockSpec((tq, LANES), lambda b, i: (b * nq + i, 0)),
        ],
        out_specs=pl.BlockSpec((tq, qw), lambda b, i: (b * nq + i, 0)),
        out_shape=jax.ShapeDtypeStruct((T, qw), BF16),
        scratch_shapes=[
            pltpu.VMEM((nq, tq, tq), I32),
            pltpu.VMEM((nq, tq, tq), F32),
            pltpu.VMEM((tq, LANES), I32),
            pltpu.VMEM((tq, LANES), I32),
            pltpu.VMEM((A_HEADS, tq, LANES), F32),
            pltpu.VMEM((A_HEADS, tq, 2 * A_HEAD_DIM), F32),
        ],
        compiler_params=_cparams(("arbitrary", "arbitrary")),
        name="dsa_attn",
    )(proj, proj, proj, proj, kv, g)


def _mlstm_body(q_ref, k_ref, v_ref, o_ref, gt_ref, y_ref, c_ref, n_ref, m_ref, *, L):
    H = M_HEADS

    @pl.when(pl.program_id(1) == 0)
    def _():
        c_ref[...] = jnp.zeros_like(c_ref)
        n_ref[...] = jnp.zeros_like(n_ref)
        m_ref[...] = jnp.zeros_like(m_ref)

    li = gt_ref[G_MI:G_MI + H, :]
    gf = gt_ref[G_MF:G_MF + H, :]
    lf = jnp.minimum(gf, 0.0) - jnp.log1p(jnp.exp(-jnp.abs(gf)))
    lane = lax.broadcasted_iota(I32, (H, L), 1)
    bcum = lf
    sh = 1
    while sh < L:
        bcum = bcum + jnp.where(lane >= sh, pltpu.roll(bcum, sh, axis=1), 0.0)
        sh *= 2
    b_last = bcum[:, L - 1:L]
    w_end = b_last - bcum + li
    a_end = jnp.max(w_end, axis=-1, keepdims=True)
    wexp = jnp.exp(w_end - a_end)
    m_old = m_ref[:, 0:1]
    m_new = jnp.maximum(b_last + m_old, a_end)
    decay = jnp.exp(b_last + m_old - m_new)
    scl = jnp.exp(a_end - m_new)
    inter_log = bcum + m_old
    crow = li - bcum
    m_ref[...] = jnp.broadcast_to(m_new, m_ref.shape)

    stack = jnp.concatenate([bcum, wexp, inter_log, jnp.zeros((LANES - 3 * H, L), F32)], axis=0)
    cols = stack.T

    tri = lax.broadcasted_iota(I32, (L, L), 1) <= lax.broadcasted_iota(I32, (L, L), 0)
    sc = M_QK_DIM ** -0.5
    for h in range(H):
        qh = q_ref[:, h * M_QK_DIM:(h + 1) * M_QK_DIM]
        kh = k_ref[:, h * M_QK_DIM:(h + 1) * M_QK_DIM]
        vh = v_ref[:, h * M_V_DIM:(h + 1) * M_V_DIM]
        oh = o_ref[:, h * M_V_DIM:(h + 1) * M_V_DIM]
        bcum_c = cols[:, h:h + 1]
        wexp_c = cols[:, H + h:H + h + 1]
        ilog_c = cols[:, 2 * H + h:2 * H + h + 1]
        C = c_ref[h]
        nrow = n_ref[h:h + 1, :]

        inter_num = jnp.dot(qh, C.astype(BF16), preferred_element_type=F32) * sc
        inter_den = jnp.sum(qh.astype(F32) * nrow, axis=-1, keepdims=True) * sc
        qk = lax.dot_general(qh, kh, _NT, preferred_element_type=F32) * sc
        dmat = jnp.where(tri, bcum_c + crow[h:h + 1, :], -jnp.inf)
        m_row = jnp.maximum(ilog_c, jnp.max(dmat, axis=-1, keepdims=True))
        inter_w = jnp.exp(ilog_c - m_row)
        intra_w = jnp.exp(dmat - m_row) * qk
        num = inter_w * inter_num + jnp.dot(intra_w.astype(BF16), vh, preferred_element_type=F32)
        den = inter_w * inter_den + jnp.sum(intra_w, axis=-1, keepdims=True)
        hout = num / jnp.maximum(jnp.abs(den), jnp.exp(-m_row))
        y_ref[:, h * M_V_DIM:(h + 1) * M_V_DIM] = (hout * jax.nn.sigmoid(oh.astype(F32))).astype(BF16)

        kw = kh.astype(F32) * wexp_c
        d_h = decay[h:h + 1, :]
        s_h = scl[h:h + 1, :]
        upd = lax.dot_general(kw.astype(BF16), vh, _TN, preferred_element_type=F32)
        c_ref[h] = d_h * C + s_h * upd
        n_ref[h:h + 1, :] = d_h * nrow + s_h * jnp.sum(kw, axis=0, keepdims=True)


def _mlstm(proj, gt, B, S, L):
    T = proj.shape[0]
    nc = S // L
    qw = M_HEADS * M_QK_DIM
    vw = M_HEADS * M_V_DIM
    return pl.pallas_call(
        functools.partial(_mlstm_body, L=L),
        grid=(B, nc),
        in_specs=[
            pl.BlockSpec((L, qw), lambda b, c: (b * nc + c, C_MQ // qw)),
            pl.BlockSpec((L, qw), lambda b, c: (b * nc + c, C_MK // qw)),
            pl.BlockSpec((L, vw), lambda b, c: (b * nc + c, C_MV // vw)),
            pl.BlockSpec((L, vw), lambda b, c: (b * nc + c, C_MO // vw)),
            pl.BlockSpec((LANES, L), lambda b, c: (0, b * nc + c)),
        ],
        out_specs=pl.BlockSpec((L, vw), lambda b, c: (b * nc + c, 0)),
        out_shape=jax.ShapeDtypeStruct((T, vw), BF16),
        scratch_shapes=[
            pltpu.VMEM((M_HEADS, M_QK_DIM, M_V_DIM), F32),
            pltpu.VMEM((M_HEADS, M_QK_DIM), F32),
            pltpu.VMEM((M_HEADS, LANES), F32),
        ],
        compiler_params=_cparams(("arbitrary", "arbitrary")),
        name="mlstm",
    )(proj, proj, proj, proj, gt)


def _mix_body(ya_ref, ym_ref, wa_ref, wm_ref, ga_ref, gm_ref, o_ref):
    a = jnp.dot(ya_ref[...], wa_ref[...], preferred_element_type=F32)
    m = jnp.dot(ym_ref[...], wm_ref[...], preferred_element_type=F32)
    mix = jax.nn.sigmoid(ga_ref[...].astype(F32)) * a + jax.nn.sigmoid(gm_ref[...].astype(F32)) * m
    o_ref[...] = mix.astype(BF16)


def _mix(ya, ym, wa, wm, proj, tm, tn):
    T = ya.shape[0]
    return pl.pallas_call(
        _mix_body,
        grid=(D_MODEL // tn, T // tm),
        in_specs=[
            pl.BlockSpec((tm, ya.shape[1]), lambda j, i: (i, 0)),
            pl.BlockSpec((tm, ym.shape[1]), lambda j, i: (i, 0)),
            pl.BlockSpec((wa.shape[0], tn), lambda j, i: (0, j)),
            pl.BlockSpec((wm.shape[0], tn), lambda j, i: (0, j)),
            pl.BlockSpec((tm, tn), lambda j, i: (i, C_GA // tn + j)),
            pl.BlockSpec((tm, tn), lambda j, i: (i, C_GM // tn + j)),
        ],
        out_specs=pl.BlockSpec((tm, tn), lambda j, i: (i, j)),
        out_shape=jax.ShapeDtypeStruct((T, D_MODEL), BF16),
        compiler_params=_cparams(("arbitrary", "arbitrary")),
        name="mix",
    )(ya, ym, wa, wm, proj, proj)


def _layer_norm(z, g, b):
    mu = jnp.mean(z, axis=-1, keepdims=True)
    d = z - mu
    var = jnp.mean(d * d, axis=-1, keepdims=True)
    return d * lax.rsqrt(var + LN_EPS) * g + b


def _first_max_index(vals):
    vmax = vals[0]
    for v in vals[1:]:
        vmax = jnp.maximum(vmax, v)
    idx = jnp.full(vmax.shape, len(vals) - 1, I32)
    for k in range(len(vals) - 2, -1, -1):
        idx = jnp.where(vals[k] == vmax, k, idx)
    return idx, vmax


def _ln1_body(mix_ref, h_ref, wo_ref, g_ref, b_ref, wrt_ref, brt_ref, h1_ref, route_ref, rt_ref, *, tm):
    z = DN_ALPHA * h_ref[...] + jnp.dot(mix_ref[...], wo_ref[...], preferred_element_type=F32)
    h1 = _layer_norm(z, g_ref[...], b_ref[...])
    h1_ref[...] = h1

    r = lax.dot_general(wrt_ref[...], h1.astype(BF16), _NT, preferred_element_type=F32) + brt_ref[...]
    gl = [r[k:k + 1, :] for k in range(N_GROUPS)]
    g_idx, g_max = _first_max_index(gl)
    g_sum = gl[0] * 0.0
    for v in gl:
        g_sum = g_sum + jnp.exp(v - g_max)
    g_p = 1.0 / g_sum
    el = []
    for e in range(EXPERTS_PER_GROUP):
        rows_e = [N_GROUPS + g * EXPERTS_PER_GROUP + e for g in range(N_GROUPS)]
        v = r[rows_e[-1]:rows_e[-1] + 1, :]
        for g in range(N_GROUPS - 2, -1, -1):
            v = jnp.where(g_idx == g, r[rows_e[g]:rows_e[g] + 1, :], v)
        el.append(v)
    _, e_max = _first_max_index(el)
    ex = [jnp.exp(v - e_max) for v in el]
    e_sum = ex[0] + ex[1] + ex[2] + ex[3]
    ep = [v / e_sum for v in ex]
    i1, p1 = _first_max_index(ep)
    rest = [jnp.where(i1 == e, -1.0, ep[e]) for e in range(EXPERTS_PER_GROUP)]
    i2, p2 = _first_max_index(rest)
    den = p1 + p2
    rows = []
    for e in range(EXPERTS_PER_GROUP):
        w = jnp.where(i1 == e, p1 / den, jnp.where(i2 == e, p2 / den, 0.0))
        rows.append(g_p * w)
    rows.append(g_idx.astype(F32))
    rows.append(jnp.zeros((8 - len(rows), tm), F32))
    route_t = jnp.concatenate(rows, axis=0)
    rt_ref[...] = route_t
    full = jnp.concatenate([route_t, jnp.zeros((LANES - 8, tm), F32)], axis=0)
    route_ref[...] = full.T


def _ln1(mix, h, wo, g, b, wrt, brt, tm):
    T = mix.shape[0]
    return pl.pallas_call(
        functools.partial(_ln1_body, tm=tm),
        grid=(T // tm,),
        in_specs=[
            pl.BlockSpec((tm, D_MODEL), lambda i: (i, 0)),
            pl.BlockSpec((tm, D_MODEL), lambda i: (i, 0)),
            pl.BlockSpec((D_MODEL, D_MODEL), lambda i: (0, 0)),
            pl.BlockSpec((1, D_MODEL), lambda i: (0, 0)),
            pl.BlockSpec((1, D_MODEL), lambda i: (0, 0)),
            pl.BlockSpec((LANES, D_MODEL), lambda i: (0, 0)),
            pl.BlockSpec((LANES, 1), lambda i: (0, 0)),
        ],
        out_specs=[
            pl.BlockSpec((tm, D_MODEL), lambda i: (i, 0)),
            pl.BlockSpec((tm, LANES), lambda i: (i, 0)),
            pl.BlockSpec((8, tm), lambda i: (0, i)),
        ],
        out_shape=[
            jax.ShapeDtypeStruct((T, D_MODEL), F32),
            jax.ShapeDtypeStruct((T, LANES), F32),
            jax.ShapeDtypeStruct((8, T), F32),
        ],
        compiler_params=_cparams(("arbitrary",)),
        name="ln1_router",
    )(mix, h, wo, g, b, wrt, brt)


META_START, META_COUNT, META_TOTAL = 0, N_GROUPS, 2 * N_GROUPS


def _pos_body(gid_ref, pos_ref, meta_ref, *, tile, R):
    gid = gid_ref[...]
    upper = (lax.broadcasted_iota(I32, (LANES, LANES), 0) <= lax.broadcasted_iota(I32, (LANES, LANES), 1))
    upper = jnp.where(upper, 1.0, 0.0).astype(BF16)
    below = lax.broadcasted_iota(I32, (R, R), 1) < lax.broadcasted_iota(I32, (R, R), 0)
    below = jnp.where(below, 1.0, 0.0).astype(BF16)
    lane = lax.broadcasted_iota(I32, (1, LANES), 1)
    pos = jnp.zeros((R, LANES), F32)
    meta = jnp.zeros((1, LANES), F32)
    start = jnp.zeros((1, 1), F32)
    for g in range(N_GROUPS):
        oh = gid == float(g)
        incl = jnp.dot(jnp.where(oh, 1.0, 0.0).astype(BF16), upper, preferred_element_type=F32)
        rowtot = incl[:, LANES - 1:LANES]
        before = jnp.dot(below, jnp.broadcast_to(rowtot, (R, LANES)).astype(BF16), preferred_element_type=F32)
        cnt = jnp.sum(rowtot, axis=0, keepdims=True)
        pos = jnp.where(oh, start + before + incl - 1.0, pos)
        meta = jnp.where(lane == META_START + g, start, meta)
        meta = jnp.where(lane == META_COUNT + g, cnt, meta)
        start = start + jnp.floor((cnt + (tile - 1)) * (1.0 / tile)) * tile
    meta = jnp.where(lane == META_TOTAL, start, meta)
    pos_ref[...] = pos.astype(I32)
    meta_ref[...] = meta.astype(I32)


def _positions(gid, tile):
    R = gid.shape[0]
    return pl.pallas_call(
        functools.partial(_pos_body, tile=tile, R=R),
        out_shape=[jax.ShapeDtypeStruct((R, LANES), I32), jax.ShapeDtypeStruct((1, LANES), I32)],
        name="positions",
    )(gid)


SUB = D_MODEL // LANES


def _store_token_major(dst_ref, val, n):
    for s in range(SUB):
        dst_ref[pl.ds(s, n, stride=SUB), :] = val[:, s * LANES:(s + 1) * LANES]


def _token_block(ref, t):
    return ref.at[pl.ds(pl.multiple_of(t * SUB, SUB), SUB), :]


def _scatter_body(pos_ref, meta_ref, h_ref, route_ref, xs_ref, rs_ref, xbuf, zx_ref, zr_ref, xsem, rsem, *, nb):
    i = pl.program_id(0)
    base = i * nb

    @pl.when(i == 0)
    def _():
        zx_ref[...] = jnp.zeros_like(zx_ref)
        zr_ref[...] = jnp.zeros_like(zr_ref)
        for g in range(N_GROUPS + 1):
            if g < N_GROUPS:
                lo = meta_ref[META_START + g] + meta_ref[META_COUNT + g]
                hi = meta_ref[META_START + g + 1] if g + 1 < N_GROUPS else meta_ref[META_TOTAL]
            else:
                lo, hi = meta_ref[META_TOTAL], rs_ref.shape[0]

            def zcopies(r):
                return (pltpu.make_async_copy(zx_ref, _token_block(xs_ref, r), xsem),
                        pltpu.make_async_copy(zr_ref, rs_ref.at[pl.ds(r, 1), :], rsem))

            def zstart(r, c):
                for cp in zcopies(r):
                    cp.start()
                return c

            def zwait(r, c):
                for cp in zcopies(r):
                    cp.wait()
                return c

            lax.fori_loop(lo, hi, zstart, 0)
            lax.fori_loop(lo, hi, zwait, 0)

    _store_token_major(xbuf, h_ref[...], nb)

    def copies(t):
        dst = pos_ref[base + t]
        return (pltpu.make_async_copy(_token_block(xbuf, t), _token_block(xs_ref, dst), xsem),
                pltpu.make_async_copy(route_ref.at[pl.ds(t, 1), :], rs_ref.at[pl.ds(dst, 1), :], rsem))

    def start(t, c):
        for prio, cp in enumerate(copies(t)):
            cp.start(priority=prio)
        return c

    def wait(t, c):
        for cp in copies(t):
            cp.wait()
        return c

    lax.fori_loop(0, nb, start, 0)
    lax.fori_loop(0, nb, wait, 0)


def _scatter_rows(pos, meta, h1, route, n_out, nb):
    T = h1.shape[0]
    return pl.pallas_call(
        functools.partial(_scatter_body, nb=nb),
        grid_spec=pltpu.PrefetchScalarGridSpec(
            num_scalar_prefetch=2,
            grid=(T // nb,),
            in_specs=[
                pl.BlockSpec((nb, D_MODEL), lambda i, pos_ref, meta_ref: (i, 0)),
                pl.BlockSpec((nb, LANES), lambda i, pos_ref, meta_ref: (i, 0)),
            ],
            out_specs=[pl.BlockSpec(memory_space=pl.ANY), pl.BlockSpec(memory_space=pl.ANY)],
            scratch_shapes=[
                pltpu.VMEM((nb * SUB, LANES), F32),
                pltpu.VMEM((SUB, LANES), F32),
                pltpu.VMEM((1, LANES), F32),
                pltpu.SemaphoreType.DMA(()),
                pltpu.SemaphoreType.DMA(()),
            ],
        ),
        out_shape=[jax.ShapeDtypeStruct((n_out * SUB, LANES), F32), jax.ShapeDtypeStruct((n_out, LANES), F32)],
        compiler_params=pltpu.CompilerParams(dimension_semantics=("arbitrary",), has_side_effects=True,
                                             vmem_limit_bytes=VMEM_LIMIT),
        name="scatter_rows",
    )(pos, meta, h1, route)


def _tile_group(meta_ref, i, tile):
    row = i * tile
    g = jnp.int32(0)
    for k in range(1, N_GROUPS):
        g = g + (row >= meta_ref[META_START + k]).astype(I32)
    return g


def _moe_body(meta_ref, xs_ref, rs_ref, wg_ref, wu_ref, wd_ref, ys_ref, xb_ref, acc_ref, *, tile):
    i = pl.program_id(0)
    e = pl.program_id(1)
    last_e = EXPERTS_PER_GROUP - 1
    valid = i * tile < meta_ref[META_TOTAL]

    @pl.when(jnp.logical_and(valid, e == 0))
    def _():
        for s in range(SUB):
            xb_ref[:, s * LANES:(s + 1) * LANES] = xs_ref[pl.ds(s, tile, stride=SUB), :].astype(BF16)

    @pl.when(jnp.logical_and(jnp.logical_not(valid), e == 0))
    def _():
        ys_ref[...] = jnp.zeros_like(ys_ref)

    @pl.when(valid)
    def _():
        x = xb_ref[...]
        hg = jnp.dot(x, wg_ref[0], preferred_element_type=F32)
        hu = jnp.dot(x, wu_ref[0], preferred_element_type=F32)
        route = rs_ref[...]
        lane = lax.broadcasted_iota(I32, route.shape, 1)
        cw = jnp.sum(jnp.where(lane == e, route, 0.0), axis=-1, keepdims=True)
        hmid = (hg * jax.nn.sigmoid(hg)) * hu * cw
        y = jnp.dot(hmid.astype(BF16), wd_ref[0], preferred_element_type=F32)

        @pl.when(e == 0)
        def _():
            acc_ref[...] = y

        @pl.when(jnp.logical_and(e > 0, e < last_e))
        def _():
            acc_ref[...] += y

        @pl.when(e == last_e)
        def _():
            _store_token_major(ys_ref, acc_ref[...] + y, tile)


def _moe(meta, xs, rs, wg, wu, wd, tile):
    n_rows = rs.shape[0]
    nt = n_rows // tile

    def row_map(i, e, meta_ref):
        last = jnp.maximum(meta_ref[META_TOTAL] // tile - 1, 0)
        return (jnp.minimum(i, last), 0)

    def w_map(i, e, meta_ref):
        last = jnp.maximum(meta_ref[META_TOTAL] // tile - 1, 0)
        g = _tile_group(meta_ref, jnp.minimum(i, last), tile)
        return (g * EXPERTS_PER_GROUP + e, 0, 0)

    return pl.pallas_call(
        functools.partial(_moe_body, tile=tile),
        grid_spec=pltpu.PrefetchScalarGridSpec(
            num_scalar_prefetch=1,
            grid=(nt, EXPERTS_PER_GROUP),
            in_specs=[
                pl.BlockSpec((tile * SUB, LANES), row_map),
                pl.BlockSpec((tile, LANES), row_map),
                pl.BlockSpec((1, D_MODEL, EXPERT_FF), w_map),
                pl.BlockSpec((1, D_MODEL, EXPERT_FF), w_map),
                pl.BlockSpec((1, EXPERT_FF, D_MODEL), w_map),
            ],
            out_specs=pl.BlockSpec((tile * SUB, LANES), lambda i, e, meta_ref: (i, 0)),
            scratch_shapes=[pltpu.VMEM((tile, D_MODEL), BF16), pltpu.VMEM((tile, D_MODEL), F32)],
        ),
        out_shape=jax.ShapeDtypeStruct((n_rows * SUB, LANES), F32),
        compiler_params=_cparams(("arbitrary", "arbitrary")),
        name="moe",
    )(meta, xs, rs, wg, wu, wd)


def _ln2_body(pos_ref, h_ref, ys_ref, p_ref, wp_ref, wpg_ref, g_ref, b_ref, o_ref, fbuf, sem, *, tm):
    i = pl.program_id(0)
    slot = i % 2

    def copy(step, t, sl):
        return pltpu.make_async_copy(_token_block(ys_ref, pos_ref[step * tm + t]), _token_block(fbuf.at[sl], t),
                                     sem.at[sl])

    def issue(step, sl):
        def body(j, c):
            for k in range(2):
                copy(step, 2 * j + k, sl).start(priority=k)
            return c

        lax.fori_loop(0, tm // 2, body, 0)

    @pl.when(i == 0)
    def _():
        issue(0, 0)

    @pl.when(i + 1 < pl.num_programs(0))
    def _():
        issue(i + 1, 1 - slot)

    h1 = h_ref[...]
    ple = jnp.dot(p_ref[...].astype(BF16), wp_ref[...], preferred_element_type=F32)
    gate = jnp.dot(h1.astype(BF16), wpg_ref[...], preferred_element_type=F32)
    z = DN_ALPHA * h1 + ple * jax.nn.sigmoid(gate)

    def wait(t, c):
        copy(i, t, slot).wait()
        return c

    lax.fori_loop(0, tm, wait, 0)
    ff = jnp.concatenate([fbuf[slot, pl.ds(s, tm, stride=SUB), :] for s in range(SUB)], axis=1)
    o_ref[...] = _layer_norm(z + ff, g_ref[...], b_ref[...])


def _ln2(pos, h1, ys, p, wple, wpg, g, b, tm):
    T = h1.shape[0]
    return pl.pallas_call(
        functools.partial(_ln2_body, tm=tm),
        grid_spec=pltpu.PrefetchScalarGridSpec(
            num_scalar_prefetch=1,
            grid=(T // tm,),
            in_specs=[
                pl.BlockSpec((tm, D_MODEL), lambda i, pos_ref: (i, 0)),
                pl.BlockSpec(memory_space=pl.ANY),
                pl.BlockSpec((tm, PLE_DIM), lambda i, pos_ref: (i, 0)),
                pl.BlockSpec((PLE_DIM, D_MODEL), lambda i, pos_ref: (0, 0)),
                pl.BlockSpec((D_MODEL, D_MODEL), lambda i, pos_ref: (0, 0)),
                pl.BlockSpec((1, D_MODEL), lambda i, pos_ref: (0, 0)),
                pl.BlockSpec((1, D_MODEL), lambda i, pos_ref: (0, 0)),
            ],
            out_specs=pl.BlockSpec((tm, D_MODEL), lambda i, pos_ref: (i, 0)),
            scratch_shapes=[pltpu.VMEM((2, tm * SUB, LANES), F32), pltpu.SemaphoreType.DMA((2,))],
        ),
        out_shape=jax.ShapeDtypeStruct((T, D_MODEL), F32),
        compiler_params=_cparams(("arbitrary",)),
        name="ple_ln2",
    )(pos, h1, ys, p, wple, wpg, g, b)


def _pack_in_weights(w_in_l, m_i_bias_l, m_f_bias_l):
    parts, off = {}, 0
    for name, n in _SPLITS:
        parts[name] = w_in_l[:, off:off + n]
        off += n
    z64 = jnp.zeros((D_MODEL, IDX_DIM), w_in_l.dtype)
    wp = jnp.concatenate(
        [parts["a_q"], parts["m_q"], parts["m_k"], parts["m_v"], parts["m_o"], parts["g_a"], parts["g_m"],
         parts["i_q"], parts["a_ckv"], parts["i_k"], z64, z64, parts["i_k"]], axis=1).astype(BF16)
    pad = jnp.zeros((D_MODEL, LANES - IDX_HEADS - 2 * M_HEADS), w_in_l.dtype)
    ws = jnp.concatenate([parts["i_w"], parts["m_i"], parts["m_f"], pad], axis=1).astype(BF16)
    bias = jnp.concatenate([jnp.zeros((IDX_HEADS,), F32), m_i_bias_l.astype(F32), m_f_bias_l.astype(F32),
                            jnp.zeros((LANES - IDX_HEADS - 2 * M_HEADS,), F32)])
    return wp, ws, ws.T, bias.reshape(1, LANES), bias.reshape(LANES, 1)


def _tiles(T, S):
    return dict(
        inproj_tm=min(1024, T), inproj_tn=1536, row_tm=min(512, T), mix_tn=1024,
        attn_tq=min(256, S), mlstm_l=min(256, S), moe_tile=min(512, T // 4), dma_rows=min(256, T),
    )


def kernel(x, p, w_in, kv_norm, w_ukv, m_i_bias, m_f_bias, w_proj_a, w_proj_m, w_out, ln1_g, ln1_b, w_grp, b_grp,
           w_exp, b_exp, w_gate, w_up, w_down, w_ple, w_ple_gate, ln2_g, ln2_b):
    B, S, D = x.shape
    T = B * S
    assert D == D_MODEL and T % LANES == 0
    cfg = _tiles(T, S)
    h = x.reshape(T, D)
    for l in range(w_in.shape[0]):
        wp, ws, wst, brow, bcol = _pack_in_weights(w_in[l], m_i_bias[l], m_f_bias[l])
        proj, g, gt = _inproj(h, wp, ws, wst, brow, bcol, cfg["inproj_tm"], cfg["inproj_tn"])
        kv = _kv(proj, kv_norm[l].reshape(1, -1).astype(F32), w_ukv[l].astype(BF16), cfg["row_tm"])
        y_a = _attn(proj, kv, g, B, S, cfg["attn_tq"])
        y_m = _mlstm(proj, gt, B, S, cfg["mlstm_l"])
        mix = _mix(y_a, y_m, w_proj_a[l].astype(BF16), w_proj_m[l].astype(BF16), proj, cfg["row_tm"], cfg["mix_tn"])

        n_route = N_GROUPS + N_EXPERTS
        wr = jnp.concatenate([w_grp[l], w_exp[l], jnp.zeros((D, LANES - n_route), F32)], axis=1)
        br = jnp.concatenate([b_grp[l], b_exp[l], jnp.zeros((LANES - n_route,), F32)])
        h1, route, rt = _ln1(mix, h, w_out[l].astype(BF16), ln1_g[l].reshape(1, D), ln1_b[l].reshape(1, D),
                             wr.T.astype(BF16), br.reshape(LANES, 1), cfg["row_tm"])

        tile = cfg["moe_tile"]
        pos, meta = _positions(rt[N_GROUPS].reshape(T // LANES, LANES), tile)
        pos = pos.reshape(T)
        meta = meta.reshape(LANES)
        xs, rs = _scatter_rows(pos, meta, h1, route, T + N_GROUPS * tile, cfg["dma_rows"])
        ys = _moe(meta, xs, rs, w_gate[l].astype(BF16), w_up[l].astype(BF16), w_down[l].astype(BF16), tile)
        h = _ln2(pos, h1, ys, p[l].reshape(T, PLE_DIM), w_ple[l].astype(BF16), w_ple_gate[l].astype(BF16),
                 ln2_g[l].reshape(1, D), ln2_b[l].reshape(1, D), cfg["row_tm"])
    return h.reshape(B, S, D)
```

```python
import functools

import jax
import jax.numpy as jnp
from jax import lax
from jax.experimental import pallas as pl
from jax.experimental.pallas import tpu as pltpu

F32 = jnp.float32
BF16 = jnp.bfloat16
I32 = jnp.int32

D_MODEL = 2048
DEPTH = 2
A_HEADS = 16
A_HEAD_DIM = 128
A_KV_RANK = 256
IDX_HEADS = 16
IDX_DIM = 64
TOPK_MAX = 256
M_HEADS = 8
M_QK_DIM = 128
M_V_DIM = 256
N_GROUPS = 4
EXPERTS_PER_GROUP = 4
N_EXPERTS = N_GROUPS * EXPERTS_PER_GROUP
EXPERT_FF = 512
PLE_DIM = 256
LN_EPS = 1e-5
DN_ALPHA = (2 * DEPTH) ** 0.25

_SPLITS = (
    ("a_q", A_HEADS * A_HEAD_DIM), ("a_ckv", A_KV_RANK), ("i_q", IDX_HEADS * IDX_DIM), ("i_k", IDX_DIM),
    ("i_w", IDX_HEADS), ("m_q", M_HEADS * M_QK_DIM), ("m_k", M_HEADS * M_QK_DIM), ("m_v", M_HEADS * M_V_DIM),
    ("m_o", M_HEADS * M_V_DIM), ("m_i", M_HEADS), ("m_f", M_HEADS), ("g_a", D_MODEL), ("g_m", D_MODEL),
)

C_AQ, C_MQ, C_MK, C_MV, C_MO, C_GA, C_GM = 0, 2048, 3072, 4096, 6144, 8192, 10240
C_IQ, C_CKV, C_IKA, C_IKB = 12288, 13312, 13568, 13696
N_MAIN = 13824
G_IW, G_MI, G_MF = 0, 16, 24
LANES = 128

INT_MIN = -(2 ** 31)
NEG_BIG = -1e30
LOG2_E = 1.4426950408889634
VMEM_LIMIT = 56 * 1024 * 1024

_NT = (((1,), (1,)), ((), ()))
_TN = (((0,), (0,)), ((), ()))


def _cparams(sem):
    return pltpu.CompilerParams(dimension_semantics=sem, vmem_limit_bytes=VMEM_LIMIT)


def _inproj_body(x_ref, w_ref, ws_ref, wst_ref, brow_ref, bcol_ref, o_ref, g_ref, gt_ref, xb_ref):
    @pl.when(pl.program_id(1) == 0)
    def _():
        xb = x_ref[...].astype(BF16)
        xb_ref[...] = xb
        g_ref[...] = jnp.dot(xb, ws_ref[...], preferred_element_type=F32) + brow_ref[...]
        gt_ref[...] = lax.dot_general(wst_ref[...], xb, _NT, preferred_element_type=F32) + bcol_ref[...]

    o_ref[...] = jnp.dot(xb_ref[...], w_ref[...], preferred_element_type=F32).astype(BF16)


def _inproj(x, wp, ws, wst, brow, bcol, tm, tn):
    T = x.shape[0]
    return pl.pallas_call(
        _inproj_body,
        grid=(T // tm, N_MAIN // tn),
        in_specs=[
            pl.BlockSpec((tm, D_MODEL), lambda i, j: (i, 0)),
            pl.BlockSpec((D_MODEL, tn), lambda i, j: (0, j)),
            pl.BlockSpec((D_MODEL, LANES), lambda i, j: (0, 0)),
            pl.BlockSpec((LANES, D_MODEL), lambda i, j: (0, 0)),
            pl.BlockSpec((1, LANES), lambda i, j: (0, 0)),
            pl.BlockSpec((LANES, 1), lambda i, j: (0, 0)),
        ],
        out_specs=[
            pl.BlockSpec((tm, tn), lambda i, j: (i, j)),
            pl.BlockSpec((tm, LANES), lambda i, j: (i, 0)),
            pl.BlockSpec((LANES, tm), lambda i, j: (0, i)),
        ],
        out_shape=[
            jax.ShapeDtypeStruct((T, N_MAIN), BF16),
            jax.ShapeDtypeStruct((T, LANES), F32),
            jax.ShapeDtypeStruct((LANES, T), F32),
        ],
        scratch_shapes=[pltpu.VMEM((tm, D_MODEL), BF16)],
        compiler_params=_cparams(("arbitrary", "arbitrary")),
        name="inproj",
    )(x, wp, ws, wst, brow, bcol)


def _kv_body(c_ref, gn_ref, w_ref, o_ref):
    c = c_ref[...].astype(F32)
    r = c * lax.rsqrt(jnp.mean(c * c, axis=-1, keepdims=True) + LN_EPS) * gn_ref[...]
    o_ref[...] = jnp.dot(r.astype(BF16), w_ref[...], preferred_element_type=F32).astype(BF16)


def _kv(proj, gn, wukv, tm):
    T = proj.shape[0]
    return pl.pallas_call(
        _kv_body,
        grid=(T // tm,),
        in_specs=[
            pl.BlockSpec((tm, A_KV_RANK), lambda i: (i, C_CKV // A_KV_RANK)),
            pl.BlockSpec((1, A_KV_RANK), lambda i: (0, 0)),
            pl.BlockSpec((A_KV_RANK, 2 * A_HEAD_DIM), lambda i: (0, 0)),
        ],
        out_specs=pl.BlockSpec((tm, 2 * A_HEAD_DIM), lambda i: (i, 0)),
        out_shape=jax.ShapeDtypeStruct((T, 2 * A_HEAD_DIM), BF16),
        compiler_params=_cparams(("arbitrary",)),
        name="kv",
    )(proj, gn, wukv)


def _attn_body(q_ref, iq_ref, ika_ref, ikb_ref, kv_ref, g_ref, o_ref, key_ref, mb_ref, cand_ref, ans_ref, m_ref,
               acc_ref, *, tq, n_keep):
    qi = pl.program_id(1)
    nck = qi + 1
    t0 = qi * tq
    row_t = t0 + lax.broadcasted_iota(I32, (tq, tq), 0)
    col_l = lax.broadcasted_iota(I32, (tq, tq), 1)

    wq = g_ref[:, G_IW:G_IW + IDX_HEADS] * (IDX_HEADS ** -0.5 * IDX_DIM ** -0.5)

    def score_chunk(c, carry):
        k0 = pl.multiple_of(c * tq, tq)
        ka = ika_ref[pl.ds(k0, tq), :]
        kb = ikb_ref[pl.ds(k0, tq), :]
        acc = jnp.zeros((tq, tq), F32)
        for p in range(IDX_HEADS // 2):
            pair = iq_ref[:, p * LANES:(p + 1) * LANES]
            xe = lax.dot_general(pair, ka, _NT, preferred_element_type=F32)
            xo = lax.dot_general(pair, kb, _NT, preferred_element_type=F32)
            acc = acc + wq[:, 2 * p:2 * p + 1] * jnp.maximum(xe, 0.0)
            acc = acc + wq[:, 2 * p + 1:2 * p + 2] * jnp.maximum(xo, 0.0)
        bits = lax.bitcast_convert_type(acc, I32)
        key = bits ^ ((bits >> 31) & 0x7FFFFFFF)
        key_ref[c] = jnp.where(k0 + col_l <= row_t, key, INT_MIN)
        return carry

    lax.fori_loop(0, nck, score_chunk, 0)

    rows = min(tq, LANES)

    ones_sq = jnp.ones((LANES, LANES), BF16)

    def try_candidate():
        cnts = []
        for r in range(tq // rows):
            rs = slice(r * rows, (r + 1) * rows)

            def count_chunk(c, cnt, rs=rs):
                cb = cand_ref[rs, :]
                k = key_ref[c, rs, :]
                for v in range(tq // LANES):
                    cnt = cnt + jnp.where(k[:, v * LANES:(v + 1) * LANES] >= cb, 1.0, 0.0)
                return cnt

            cnts.append(lax.fori_loop(0, nck, count_chunk, jnp.zeros((rows, LANES), F32)))
        tot = jnp.dot(jnp.concatenate(cnts, axis=0).astype(BF16), ones_sq, preferred_element_type=F32)
        ans_ref[...] = jnp.where(tot >= n_keep, cand_ref[...], ans_ref[...])

    ans_ref[...] = jnp.full((tq, LANES), INT_MIN, I32)
    cand_ref[...] = jnp.zeros((tq, LANES), I32)
    try_candidate()

    def bisect(b, carry):
        cand_ref[...] = ans_ref[...] | lax.shift_left(jnp.int32(1), 30 - b)
        try_candidate()
        return carry

    lax.fori_loop(0, 31, bisect, 0)
    tau = jnp.where(row_t[:, 0:LANES] < n_keep, INT_MIN + 1, ans_ref[...])
    tau = jnp.concatenate([tau] * (tq // LANES), axis=1)

    def mask_chunk(c, carry):
        mb_ref[c] = jnp.where(key_ref[c] >= tau, 0.0, NEG_BIG)
        return carry

    lax.fori_loop(0, nck, mask_chunk, 0)

    scale = A_HEAD_DIM ** -0.5
    c_exp = scale * LOG2_E
    m_ref[...] = jnp.full(m_ref.shape, NEG_BIG, F32)
    acc_ref[...] = jnp.zeros(acc_ref.shape, F32)
    ones = jnp.ones((tq, A_HEAD_DIM), BF16)

    def att_chunk(c, carry):
        k0 = pl.multiple_of(c * tq, tq)
        kc = kv_ref[pl.ds(k0, tq), 0:A_HEAD_DIM]
        v1 = jnp.concatenate([kv_ref[pl.ds(k0, tq), A_HEAD_DIM:2 * A_HEAD_DIM], ones], axis=1)
        rel = (k0 - t0 + lax.broadcasted_iota(I32, (1, tq), 1)).astype(F32)
        for h in range(A_HEADS):
            srow = (2.0 ** (-8.0 * (h + 1) / A_HEADS) / scale) * rel
            for r in range(tq // rows):
                rs = slice(r * rows, (r + 1) * rows)
                qh = q_ref[rs, h * A_HEAD_DIM:(h + 1) * A_HEAD_DIM]
                u = lax.dot_general(qh, kc, _NT, preferred_element_type=F32) + mb_ref[c, rs, :] + srow
                um = u[:, 0:LANES]
                for v in range(1, tq // LANES):
                    um = jnp.maximum(um, u[:, v * LANES:(v + 1) * LANES])
                m = m_ref[h, rs, :]
                m_new = jnp.maximum(m, jnp.broadcast_to(jnp.max(um, axis=-1, keepdims=True), (rows, LANES)))
                m_ref[h, rs, :] = m_new
                alpha = jnp.exp2((m - m_new) * c_exp)
                p = jnp.concatenate([jnp.exp2((u[:, v * LANES:(v + 1) * LANES] - m_new) * c_exp)
                                     for v in range(tq // LANES)], axis=1)
                pv = jnp.dot(p.astype(BF16), v1, preferred_element_type=F32)
                acc = acc_ref[h, rs, :]
                acc_ref[h, rs, :] = jnp.concatenate(
                    [alpha * acc[:, v * LANES:(v + 1) * LANES] for v in range(2 * A_HEAD_DIM // LANES)], axis=1) + pv
        return carry

    lax.fori_loop(0, nck, att_chunk, 0)
    for h in range(A_HEADS):
        num = acc_ref[h, :, 0:A_HEAD_DIM]
        den = acc_ref[h, :, A_HEAD_DIM:2 * A_HEAD_DIM]
        o_ref[:, h * A_HEAD_DIM:(h + 1) * A_HEAD_DIM] = (num / den).astype(BF16)


def _attn(proj, kv, g, B, S, tq):
    T = proj.shape[0]
    nq = S // tq
    n_keep = min(TOPK_MAX, S // 4)
    qw = A_HEADS * A_HEAD_DIM
    iw = IDX_HEADS * IDX_DIM
    return pl.pallas_call(
        functools.partial(_attn_body, tq=tq, n_keep=n_keep),
        grid=(B, nq),
        in_specs=[
            pl.BlockSpec((tq, qw), lambda b, i: (b * nq + i, C_AQ // qw)),
            pl.BlockSpec((tq, iw), lambda b, i: (b * nq + i, C_IQ // iw)),
            pl.BlockSpec((S, LANES), lambda b, i: (b, C_IKA // LANES)),
            pl.BlockSpec((S, LANES), lambda b, i: (b, C_IKB // LANES)),
            pl.BlockSpec((S, 2 * A_HEAD_DIM), lambda b, i: (b, 0)),
            pl.BlockSpec((tq, LANES), lambda b, i: (b * nq + i, 0)),
        ],
        out_specs=pl.BlockSpec((tq, qw), lambda b, i: (b * nq + i, 0)),
        out_shape=jax.ShapeDtypeStruct((T, qw), BF16),
        scratch_shapes=[
            pltpu.VMEM((nq, tq, tq), I32),
            pltpu.VMEM((nq, tq, tq), F32),
            pltpu.VMEM((tq, LANES), I32),
            pltpu.VMEM((tq, LANES), I32),
            pltpu.VMEM((A_HEADS, tq, LANES), F32),
            pltpu.VMEM((A_HEADS, tq, 2 * A_HEAD_DIM), F32),
        ],
        compiler_params=_cparams(("arbitrary", "arbitrary")),
        name="dsa_attn",
    )(proj, proj, proj, proj, kv, g)


def _mlstm_body(q_ref, k_ref, v_ref, o_ref, gt_ref, y_ref, c_ref, n_ref, m_ref, *, L):
    H = M_HEADS

    @pl.when(pl.program_id(1) == 0)
    def _():
        c_ref[...] = jnp.zeros_like(c_ref)
        n_ref[...] = jnp.zeros_like(n_ref)
        m_ref[...] = jnp.zeros_like(m_ref)

    li = gt_ref[G_MI:G_MI + H, :]
    gf = gt_ref[G_MF:G_MF + H, :]
    lf = jnp.minimum(gf, 0.0) - jnp.log1p(jnp.exp(-jnp.abs(gf)))
    lane = lax.broadcasted_iota(I32, (H, L), 1)
    bcum = lf
    sh = 1
    while sh < L:
        bcum = bcum + jnp.where(lane >= sh, pltpu.roll(bcum, sh, axis=1), 0.0)
        sh *= 2
    b_last = bcum[:, L - 1:L]
    w_end = b_last - bcum + li
    a_end = jnp.max(w_end, axis=-1, keepdims=True)
    wexp = jnp.exp(w_end - a_end)
    m_old = m_ref[:, 0:1]
    m_new = jnp.maximum(b_last + m_old, a_end)
    decay = jnp.exp(b_last + m_old - m_new)
    scl = jnp.exp(a_end - m_new)
    inter_log = bcum + m_old
    crow = li - bcum
    m_ref[...] = jnp.broadcast_to(m_new, m_ref.shape)

    stack = jnp.concatenate([bcum, wexp, inter_log, jnp.zeros((LANES - 3 * H, L), F32)], axis=0)
    cols = stack.T

    tri = lax.broadcasted_iota(I32, (L, L), 1) <= lax.broadcasted_iota(I32, (L, L), 0)
    sc = M_QK_DIM ** -0.5
    for h in range(H):
        qh = q_ref[:, h * M_QK_DIM:(h + 1) * M_QK_DIM]
        kh = k_ref[:, h * M_QK_DIM:(h + 1) * M_QK_DIM]
        vh = v_ref[:, h * M_V_DIM:(h + 1) * M_V_DIM]
        oh = o_ref[:, h * M_V_DIM:(h + 1) * M_V_DIM]
        bcum_c = cols[:, h:h + 1]
        wexp_c = cols[:, H + h:H + h + 1]
        ilog_c = cols[:, 2 * H + h:2 * H + h + 1]
        C = c_ref[h]
        nrow = n_ref[h:h + 1, :]

        inter_num = jnp.dot(qh, C.astype(BF16), preferred_element_type=F32) * sc
        inter_den = jnp.sum(qh.astype(F32) * nrow, axis=-1, keepdims=True) * sc
        qk = lax.dot_general(qh, kh, _NT, preferred_element_type=F32) * sc
        dmat = jnp.where(tri, bcum_c + crow[h:h + 1, :], -jnp.inf)
        m_row = jnp.maximum(ilog_c, jnp.max(dmat, axis=-1, keepdims=True))
        inter_w = jnp.exp(ilog_c - m_row)
        intra_w = jnp.exp(dmat - m_row) * qk
        num = inter_w * inter_num + jnp.dot(intra_w.astype(BF16), vh, preferred_element_type=F32)
        den = inter_w * inter_den + jnp.sum(intra_w, axis=-1, keepdims=True)
        hout = num / jnp.maximum(jnp.abs(den), jnp.exp(-m_row))
        y_ref[:, h * M_V_DIM:(h + 1) * M_V_DIM] = (hout * jax.nn.sigmoid(oh.astype(F32))).astype(BF16)

        kw = kh.astype(F32) * wexp_c
        d_h = decay[h:h + 1, :]
        s_h = scl[h:h + 1, :]
        upd = lax.dot_general(kw.astype(BF16), vh, _TN, preferred_element_type=F32)
        c_ref[h] = d_h * C + s_h * upd
        n_ref[h:h + 1, :] = d_h * nrow + s_h * jnp.sum(kw, axis=0, keepdims=True)


def _mlstm(proj, gt, B, S, L):
    T = proj.shape[0]
    nc = S // L
    qw = M_HEADS * M_QK_DIM
    vw = M_HEADS * M_V_DIM
    return pl.pallas_call(
        functools.partial(_mlstm_body, L=L),
        grid=(B, nc),
        in_specs=[
            pl.BlockSpec((L, qw), lambda b, c: (b * nc + c, C_MQ // qw)),
            pl.BlockSpec((L, qw), lambda b, c: (b * nc + c, C_MK // qw)),
            pl.BlockSpec((L, vw), lambda b, c: (b * nc + c, C_MV // vw)),
            pl.BlockSpec((L, vw), lambda b, c: (b * nc + c, C_MO // vw)),
            pl.BlockSpec((LANES, L), lambda b, c: (0, b * nc + c)),
        ],
        out_specs=pl.BlockSpec((L, vw), lambda b, c: (b * nc + c, 0)),
        out_shape=jax.ShapeDtypeStruct((T, vw), BF16),
        scratch_shapes=[
            pltpu.VMEM((M_HEADS, M_QK_DIM, M_V_DIM), F32),
            pltpu.VMEM((M_HEADS, M_QK_DIM), F32),
            pltpu.VMEM((M_HEADS, LANES), F32),
        ],
        compiler_params=_cparams(("arbitrary", "arbitrary")),
        name="mlstm",
    )(proj, proj, proj, proj, gt)


def _mix_body(ya_ref, ym_ref, wa_ref, wm_ref, ga_ref, gm_ref, o_ref):
    a = jnp.dot(ya_ref[...], wa_ref[...], preferred_element_type=F32)
    m = jnp.dot(ym_ref[...], wm_ref[...], preferred_element_type=F32)
    mix = jax.nn.sigmoid(ga_ref[...].astype(F32)) * a + jax.nn.sigmoid(gm_ref[...].astype(F32)) * m
    o_ref[...] = mix.astype(BF16)


def _mix(ya, ym, wa, wm, proj, tm, tn):
    T = ya.shape[0]
    return pl.pallas_call(
        _mix_body,
        grid=(D_MODEL // tn, T // tm),
        in_specs=[
            pl.BlockSpec((tm, ya.shape[1]), lambda j, i: (i, 0)),
            pl.BlockSpec((tm, ym.shape[1]), lambda j, i: (i, 0)),
            pl.BlockSpec((wa.shape[0], tn), lambda j, i: (0, j)),
            pl.BlockSpec((wm.shape[0], tn), lambda j, i: (0, j)),
            pl.BlockSpec((tm, tn), lambda j, i: (i, C_GA // tn + j)),
            pl.BlockSpec((tm, tn), lambda j, i: (i, C_GM // tn + j)),
        ],
        out_specs=pl.BlockSpec((tm, tn), lambda j, i: (i, j)),
        out_shape=jax.ShapeDtypeStruct((T, D_MODEL), BF16),
        compiler_params=_cparams(("arbitrary", "arbitrary")),
        name="mix",
    )(ya, ym, wa, wm, proj, proj)


def _layer_norm(z, g, b):
    mu = jnp.mean(z, axis=-1, keepdims=True)
    d = z - mu
    var = jnp.mean(d * d, axis=-1, keepdims=True)
    return d * lax.rsqrt(var + LN_EPS) * g + b


def _first_max_index(vals):
    vmax = vals[0]
    for v in vals[1:]:
        vmax = jnp.maximum(vmax, v)
    idx = jnp.full(vmax.shape, len(vals) - 1, I32)
    for k in range(len(vals) - 2, -1, -1):
        idx = jnp.where(vals[k] == vmax, k, idx)
    return idx, vmax


def _ln1_body(mix_ref, h_ref, wo_ref, g_ref, b_ref, wrt_ref, brt_ref, h1_ref, route_ref, rt_ref, *, tm):
    z = DN_ALPHA * h_ref[...] + jnp.dot(mix_ref[...], wo_ref[...], preferred_element_type=F32)
    h1 = _layer_norm(z, g_ref[...], b_ref[...])
    h1_ref[...] = h1

    r = lax.dot_general(wrt_ref[...], h1.astype(BF16), _NT, preferred_element_type=F32) + brt_ref[...]
    gl = [r[k:k + 1, :] for k in range(N_GROUPS)]
    g_idx, g_max = _first_max_index(gl)
    g_sum = gl[0] * 0.0
    for v in gl:
        g_sum = g_sum + jnp.exp(v - g_max)
    g_p = 1.0 / g_sum
    el = []
    for e in range(EXPERTS_PER_GROUP):
        rows_e = [N_GROUPS + g * EXPERTS_PER_GROUP + e for g in range(N_GROUPS)]
        v = r[rows_e[-1]:rows_e[-1] + 1, :]
        for g in range(N_GROUPS - 2, -1, -1):
            v = jnp.where(g_idx == g, r[rows_e[g]:rows_e[g] + 1, :], v)
        el.append(v)
    _, e_max = _first_max_index(el)
    ex = [jnp.exp(v - e_max) for v in el]
    e_sum = ex[0] + ex[1] + ex[2] + ex[3]
    ep = [v / e_sum for v in ex]
    i1, p1 = _first_max_index(ep)
    rest = [jnp.where(i1 == e, -1.0, ep[e]) for e in range(EXPERTS_PER_GROUP)]
    i2, p2 = _first_max_index(rest)
    den = p1 + p2
    rows = []
    for e in range(EXPERTS_PER_GROUP):
        w = jnp.where(i1 == e, p1 / den, jnp.where(i2 == e, p2 / den, 0.0))
        rows.append(g_p * w)
    rows.append(g_idx.astype(F32))
    rows.append(jnp.zeros((8 - len(rows), tm), F32))
    route_t = jnp.concatenate(rows, axis=0)
    rt_ref[...] = route_t
    full = jnp.concatenate([route_t, jnp.zeros((LANES - 8, tm), F32)], axis=0)
    route_ref[...] = full.T


def _ln1(mix, h, wo, g, b, wrt, brt, tm):
    T = mix.shape[0]
    return pl.pallas_call(
        functools.partial(_ln1_body, tm=tm),
        grid=(T // tm,),
        in_specs=[
            pl.BlockSpec((tm, D_MODEL), lambda i: (i, 0)),
            pl.BlockSpec((tm, D_MODEL), lambda i: (i, 0)),
            pl.BlockSpec((D_MODEL, D_MODEL), lambda i: (0, 0)),
            pl.BlockSpec((1, D_MODEL), lambda i: (0, 0)),
            pl.BlockSpec((1, D_MODEL), lambda i: (0, 0)),
            pl.BlockSpec((LANES, D_MODEL), lambda i: (0, 0)),
            pl.BlockSpec((LANES, 1), lambda i: (0, 0)),
        ],
        out_specs=[
            pl.BlockSpec((tm, D_MODEL), lambda i: (i, 0)),
            pl.BlockSpec((tm, LANES), lambda i: (i, 0)),
            pl.BlockSpec((8, tm), lambda i: (0, i)),
        ],
        out_shape=[
            jax.ShapeDtypeStruct((T, D_MODEL), F32),
            jax.ShapeDtypeStruct((T, LANES), F32),
            jax.ShapeDtypeStruct((8, T), F32),
        ],
        compiler_params=_cparams(("arbitrary",)),
        name="ln1_router",
    )(mix, h, wo, g, b, wrt, brt)


META_START, META_COUNT, META_TOTAL = 0, N_GROUPS, 2 * N_GROUPS


def _pos_body(gid_ref, pos_ref, meta_ref, *, tile, R):
    gid = gid_ref[...]
    upper = (lax.broadcasted_iota(I32, (LANES, LANES), 0) <= lax.broadcasted_iota(I32, (LANES, LANES), 1))
    upper = jnp.where(upper, 1.0, 0.0).astype(BF16)
    below = lax.broadcasted_iota(I32, (R, R), 1) < lax.broadcasted_iota(I32, (R, R), 0)
    below = jnp.where(below, 1.0, 0.0).astype(BF16)
    lane = lax.broadcasted_iota(I32, (1, LANES), 1)
    pos = jnp.zeros((R, LANES), F32)
    meta = jnp.zeros((1, LANES), F32)
    start = jnp.zeros((1, 1), F32)
    for g in range(N_GROUPS):
        oh = gid == float(g)
        incl = jnp.dot(jnp.where(oh, 1.0, 0.0).astype(BF16), upper, preferred_element_type=F32)
        rowtot = incl[:, LANES - 1:LANES]
        before = jnp.dot(below, jnp.broadcast_to(rowtot, (R, LANES)).astype(BF16), preferred_element_type=F32)
        cnt = jnp.sum(rowtot, axis=0, keepdims=True)
        pos = jnp.where(oh, start + before + incl - 1.0, pos)
        meta = jnp.where(lane == META_START + g, start, meta)
        meta = jnp.where(lane == META_COUNT + g, cnt, meta)
        start = start + jnp.floor((cnt + (tile - 1)) * (1.0 / tile)) * tile
    meta = jnp.where(lane == META_TOTAL, start, meta)
    pos_ref[...] = pos.astype(I32)
    meta_ref[...] = meta.astype(I32)


def _positions(gid, tile):
    R = gid.shape[0]
    return pl.pallas_call(
        functools.partial(_pos_body, tile=tile, R=R),
        out_shape=[jax.ShapeDtypeStruct((R, LANES), I32), jax.ShapeDtypeStruct((1, LANES), I32)],
        name="positions",
    )(gid)


SUB = D_MODEL // LANES


def _store_token_major(dst_ref, val, n):
    for s in range(SUB):
        dst_ref[pl.ds(s, n, stride=SUB), :] = val[:, s * LANES:(s + 1) * LANES]


def _token_block(ref, t):
    return ref.at[pl.ds(pl.multiple_of(t * SUB, SUB), SUB), :]


def _scatter_body(pos_ref, meta_ref, h_ref, route_ref, xs_ref, rs_ref, xbuf, zx_ref, zr_ref, xsem, rsem, *, nb):
    i = pl.program_id(0)
    base = i * nb

    @pl.when(i == 0)
    def _():
        zx_ref[...] = jnp.zeros_like(zx_ref)
        zr_ref[...] = jnp.zeros_like(zr_ref)
        for g in range(N_GROUPS + 1):
            if g < N_GROUPS:
                lo = meta_ref[META_START + g] + meta_ref[META_COUNT + g]
                hi = meta_ref[META_START + g + 1] if g + 1 < N_GROUPS else meta_ref[META_TOTAL]
            else:
                lo, hi = meta_ref[META_TOTAL], rs_ref.shape[0]

            def zcopies(r):
                return (pltpu.make_async_copy(zx_ref, _token_block(xs_ref, r), xsem),
                        pltpu.make_async_copy(zr_ref, rs_ref.at[pl.ds(r, 1), :], rsem))

            def zstart(r, c):
                for cp in zcopies(r):
                    cp.start()
                return c

            def zwait(r, c):
                for cp in zcopies(r):
                    cp.wait()
                return c

            lax.fori_loop(lo, hi, zstart, 0)
            lax.fori_loop(lo, hi, zwait, 0)

    _store_token_major(xbuf, h_ref[...], nb)

    def copies(t):
        dst = pos_ref[base + t]
        return (pltpu.make_async_copy(_token_block(xbuf, t), _token_block(xs_ref, dst), xsem),
                pltpu.make_async_copy(route_ref.at[pl.ds(t, 1), :], rs_ref.at[pl.ds(dst, 1), :], rsem))

    def start(t, c):
        for prio, cp in enumerate(copies(t)):
            cp.start(priority=prio)
        return c

    def wait(t, c):
        for cp in copies(t):
            cp.wait()
        return c

    lax.fori_loop(0, nb, start, 0)
    lax.fori_loop(0, nb, wait, 0)


def _scatter_rows(pos, meta, h1, route, n_out, nb):
    T = h1.shape[0]
    return pl.pallas_call(
        functools.partial(_scatter_body, nb=nb),
        grid_spec=pltpu.PrefetchScalarGridSpec(
            num_scalar_prefetch=2,
            grid=(T // nb,),
            in_specs=[
                pl.BlockSpec((nb, D_MODEL), lambda i, pos_ref, meta_ref: (i, 0)),
                pl.BlockSpec((nb, LANES), lambda i, pos_ref, meta_ref: (i, 0)),
            ],
            out_specs=[pl.BlockSpec(memory_space=pl.ANY), pl.BlockSpec(memory_space=pl.ANY)],
            scratch_shapes=[
                pltpu.VMEM((nb * SUB, LANES), F32),
                pltpu.VMEM((SUB, LANES), F32),
                pltpu.VMEM((1, LANES), F32),
                pltpu.SemaphoreType.DMA(()),
                pltpu.SemaphoreType.DMA(()),
            ],
        ),
        out_shape=[jax.ShapeDtypeStruct((n_out * SUB, LANES), F32), jax.ShapeDtypeStruct((n_out, LANES), F32)],
        compiler_params=pltpu.CompilerParams(dimension_semantics=("arbitrary",), has_side_effects=True,
                                             vmem_limit_bytes=VMEM_LIMIT),
        name="scatter_rows",
    )(pos, meta, h1, route)


def _tile_group(meta_ref, i, tile):
    row = i * tile
    g = jnp.int32(0)
    for k in range(1, N_GROUPS):
        g = g + (row >= meta_ref[META_START + k]).astype(I32)
    return g


def _moe_body(meta_ref, xs_ref, rs_ref, wg_ref, wu_ref, wd_ref, ys_ref, xb_ref, acc_ref, *, tile):
    i = pl.program_id(0)
    e = pl.program_id(1)
    last_e = EXPERTS_PER_GROUP - 1
    valid = i * tile < meta_ref[META_TOTAL]

    @pl.when(jnp.logical_and(valid, e == 0))
    def _():
        for s in range(SUB):
            xb_ref[:, s * LANES:(s + 1) * LANES] = xs_ref[pl.ds(s, tile, stride=SUB), :].astype(BF16)

    @pl.when(jnp.logical_and(jnp.logical_not(valid), e == 0))
    def _():
        ys_ref[...] = jnp.zeros_like(ys_ref)

    @pl.when(valid)
    def _():
        x = xb_ref[...]
        hg = jnp.dot(x, wg_ref[0], preferred_element_type=F32)
        hu = jnp.dot(x, wu_ref[0], preferred_element_type=F32)
        route = rs_ref[...]
        lane = lax.broadcasted_iota(I32, route.shape, 1)
        cw = jnp.sum(jnp.where(lane == e, route, 0.0), axis=-1, keepdims=True)
        hmid = (hg * jax.nn.sigmoid(hg)) * hu * cw
        y = jnp.dot(hmid.astype(BF16), wd_ref[0], preferred_element_type=F32)

        @pl.when(e == 0)
        def _():
            acc_ref[...] = y

        @pl.when(jnp.logical_and(e > 0, e < last_e))
        def _():
            acc_ref[...] += y

        @pl.when(e == last_e)
        def _():
            _store_token_major(ys_ref, acc_ref[...] + y, tile)


def _moe(meta, xs, rs, wg, wu, wd, tile):
    n_rows = rs.shape[0]
    nt = n_rows // tile

    def row_map(i, e, meta_ref):
        last = jnp.maximum(meta_ref[META_TOTAL] // tile - 1, 0)
        return (jnp.minimum(i, last), 0)

    def w_map(i, e, meta_ref):
        last = jnp.maximum(meta_ref[META_TOTAL] // tile - 1, 0)
        g = _tile_group(meta_ref, jnp.minimum(i, last), tile)
        return (g * EXPERTS_PER_GROUP + e, 0, 0)

    return pl.pallas_call(
        functools.partial(_moe_body, tile=tile),
        grid_spec=pltpu.PrefetchScalarGridSpec(
            num_scalar_prefetch=1,
            grid=(nt, EXPERTS_PER_GROUP),
            in_specs=[
                pl.BlockSpec((tile * SUB, LANES), row_map),
                pl.BlockSpec((tile, LANES), row_map),
                pl.BlockSpec((1, D_MODEL, EXPERT_FF), w_map),
                pl.BlockSpec((1, D_MODEL, EXPERT_FF), w_map),
                pl.BlockSpec((1, EXPERT_FF, D_MODEL), w_map),
            ],
            out_specs=pl.BlockSpec((tile * SUB, LANES), lambda i, e, meta_ref: (i, 0)),
            scratch_shapes=[pltpu.VMEM((tile, D_MODEL), BF16), pltpu.VMEM((tile, D_MODEL), F32)],
        ),
        out_shape=jax.ShapeDtypeStruct((n_rows * SUB, LANES), F32),
        compiler_params=_cparams(("arbitrary", "arbitrary")),
        name="moe",
    )(meta, xs, rs, wg, wu, wd)


def _ln2_body(pos_ref, h_ref, ys_ref, p_ref, wp_ref, wpg_ref, g_ref, b_ref, o_ref, fbuf, sem, *, tm):
    i = pl.program_id(0)
    slot = i % 2

    def copy(step, t, sl):
        return pltpu.make_async_copy(_token_block(ys_ref, pos_ref[step * tm + t]), _token_block(fbuf.at[sl], t),
                                     sem.at[sl])

    def issue(step, sl):
        def body(j, c):
            for k in range(2):
                copy(step, 2 * j + k, sl).start(priority=k)
            return c

        lax.fori_loop(0, tm // 2, body, 0)

    @pl.when(i == 0)
    def _():
        issue(0, 0)

    @pl.when(i + 1 < pl.num_programs(0))
    def _():
        issue(i + 1, 1 - slot)

    h1 = h_ref[...]
    ple = jnp.dot(p_ref[...].astype(BF16), wp_ref[...], preferred_element_type=F32)
    gate = jnp.dot(h1.astype(BF16), wpg_ref[...], preferred_element_type=F32)
    z = DN_ALPHA * h1 + ple * jax.nn.sigmoid(gate)

    def wait(t, c):
        copy(i, t, slot).wait()
        return c

    lax.fori_loop(0, tm, wait, 0)
    ff = jnp.concatenate([fbuf[slot, pl.ds(s, tm, stride=SUB), :] for s in range(SUB)], axis=1)
    o_ref[...] = _layer_norm(z + ff, g_ref[...], b_ref[...])


def _ln2(pos, h1, ys, p, wple, wpg, g, b, tm):
    T = h1.shape[0]
    return pl.pallas_call(
        functools.partial(_ln2_body, tm=tm),
        grid_spec=pltpu.PrefetchScalarGridSpec(
            num_scalar_prefetch=1,
            grid=(T // tm,),
            in_specs=[
                pl.BlockSpec((tm, D_MODEL), lambda i, pos_ref: (i, 0)),
                pl.BlockSpec(memory_space=pl.ANY),
                pl.BlockSpec((tm, PLE_DIM), lambda i, pos_ref: (i, 0)),
                pl.BlockSpec((PLE_DIM, D_MODEL), lambda i, pos_ref: (0, 0)),
                pl.BlockSpec((D_MODEL, D_MODEL), lambda i, pos_ref: (0, 0)),
                pl.BlockSpec((1, D_MODEL), lambda i, pos_ref: (0, 0)),
                pl.BlockSpec((1, D_MODEL), lambda i, pos_ref: (0, 0)),
            ],
            out_specs=pl.BlockSpec((tm, D_MODEL), lambda i, pos_ref: (i, 0)),
            scratch_shapes=[pltpu.VMEM((2, tm * SUB, LANES), F32), pltpu.SemaphoreType.DMA((2,))],
        ),
        out_shape=jax.ShapeDtypeStruct((T, D_MODEL), F32),
        compiler_params=_cparams(("arbitrary",)),
        name="ple_ln2",
    )(pos, h1, ys, p, wple, wpg, g, b)


def _pack_in_weights(w_in_l, m_i_bias_l, m_f_bias_l):
    w16 = lax.optimization_barrier(w_in_l.astype(BF16))
    parts, off = {}, 0
    for name, n in _SPLITS:
        parts[name] = w16[:, off:off + n]
        off += n
    z64 = jnp.zeros((D_MODEL, IDX_DIM), BF16)
    wp = jnp.concatenate(
        [parts["a_q"], parts["m_q"], parts["m_k"], parts["m_v"], parts["m_o"], parts["g_a"], parts["g_m"],
         parts["i_q"], parts["a_ckv"], parts["i_k"], z64, z64, parts["i_k"]], axis=1)
    pad = jnp.zeros((D_MODEL, LANES - IDX_HEADS - 2 * M_HEADS), BF16)
    ws = jnp.concatenate([parts["i_w"], parts["m_i"], parts["m_f"], pad], axis=1)
    bias = jnp.concatenate([jnp.zeros((IDX_HEADS,), F32), m_i_bias_l.astype(F32), m_f_bias_l.astype(F32),
                            jnp.zeros((LANES - IDX_HEADS - 2 * M_HEADS,), F32)])
    return wp, ws, ws.T, bias.reshape(1, LANES), bias.reshape(LANES, 1)


def _tiles(T, S):
    return dict(
        inproj_tm=min(1024, T), inproj_tn=1536, row_tm=min(512, T), mix_tn=1024,
        attn_tq=min(256, S), mlstm_l=min(256, S), moe_tile=min(512, T // 4), dma_rows=min(256, T),
    )


def kernel(x, p, w_in, kv_norm, w_ukv, m_i_bias, m_f_bias, w_proj_a, w_proj_m, w_out, ln1_g, ln1_b, w_grp, b_grp,
           w_exp, b_exp, w_gate, w_up, w_down, w_ple, w_ple_gate, ln2_g, ln2_b):
    B, S, D = x.shape
    T = B * S
    assert D == D_MODEL and T % LANES == 0
    cfg = _tiles(T, S)
    h = x.reshape(T, D)
    for l in range(w_in.shape[0]):
        wp, ws, wst, brow, bcol = _pack_in_weights(w_in[l], m_i_bias[l], m_f_bias[l])
        proj, g, gt = _inproj(h, wp, ws, wst, brow, bcol, cfg["inproj_tm"], cfg["inproj_tn"])
        kv = _kv(proj, kv_norm[l].reshape(1, -1).astype(F32), w_ukv[l].astype(BF16), cfg["row_tm"])
        y_a = _attn(proj, kv, g, B, S, cfg["attn_tq"])
        y_m = _mlstm(proj, gt, B, S, cfg["mlstm_l"])
        mix = _mix(y_a, y_m, w_proj_a[l].astype(BF16), w_proj_m[l].astype(BF16), proj, cfg["row_tm"], cfg["mix_tn"])

        n_route = N_GROUPS + N_EXPERTS
        wr = jnp.concatenate([w_grp[l], w_exp[l], jnp.zeros((D, LANES - n_route), F32)], axis=1)
        br = jnp.concatenate([b_grp[l], b_exp[l], jnp.zeros((LANES - n_route,), F32)])
        h1, route, rt = _ln1(mix, h, w_out[l].astype(BF16), ln1_g[l].reshape(1, D), ln1_b[l].reshape(1, D),
                             wr.T.astype(BF16), br.reshape(LANES, 1), cfg["row_tm"])

        tile = cfg["moe_tile"]
        pos, meta = _positions(rt[N_GROUPS].reshape(T // LANES, LANES), tile)
        pos = pos.reshape(T)
        meta = meta.reshape(LANES)
        xs, rs = _scatter_rows(pos, meta, h1, route, T + N_GROUPS * tile, cfg["dma_rows"])
        ys = _moe(meta, xs, rs, w_gate[l].astype(BF16), w_up[l].astype(BF16), w_down[l].astype(BF16), tile)
        h = _ln2(pos, h1, ys, p[l].reshape(T, PLE_DIM), w_ple[l].astype(BF16), w_ple_gate[l].astype(BF16),
                 ln2_g[l].reshape(1, D), ln2_b[l].reshape(1, D), cfg["row_tm"])
    return h.reshape(B, S, D)
```

```python
import functools

import jax
import jax.numpy as jnp
from jax import lax
from jax.experimental import pallas as pl
from jax.experimental.pallas import tpu as pltpu

F32 = jnp.float32
BF16 = jnp.bfloat16
I32 = jnp.int32

D_MODEL = 2048
DEPTH = 2
A_HEADS = 16
A_HEAD_DIM = 128
A_KV_RANK = 256
IDX_HEADS = 16
IDX_DIM = 64
TOPK_MAX = 256
M_HEADS = 8
M_QK_DIM = 128
M_V_DIM = 256
N_GROUPS = 4
EXPERTS_PER_GROUP = 4
N_EXPERTS = N_GROUPS * EXPERTS_PER_GROUP
EXPERT_FF = 512
PLE_DIM = 256
LN_EPS = 1e-5
DN_ALPHA = (2 * DEPTH) ** 0.25

_SPLITS = (
    ("a_q", A_HEADS * A_HEAD_DIM), ("a_ckv", A_KV_RANK), ("i_q", IDX_HEADS * IDX_DIM), ("i_k", IDX_DIM),
    ("i_w", IDX_HEADS), ("m_q", M_HEADS * M_QK_DIM), ("m_k", M_HEADS * M_QK_DIM), ("m_v", M_HEADS * M_V_DIM),
    ("m_o", M_HEADS * M_V_DIM), ("m_i", M_HEADS), ("m_f", M_HEADS), ("g_a", D_MODEL), ("g_m", D_MODEL),
)

C_AQ, C_MQ, C_MK, C_MV, C_MO, C_GA, C_GM = 0, 2048, 3072, 4096, 6144, 8192, 10240
C_IQ, C_CKV, C_IKA, C_IKB = 12288, 13312, 13568, 13696
N_MAIN = 13824
G_IW, G_MI, G_MF = 0, 16, 24
LANES = 128

INT_MIN = -(2 ** 31)
NEG_BIG = -1e30
LOG2_E = 1.4426950408889634
VMEM_LIMIT = 56 * 1024 * 1024

_NT = (((1,), (1,)), ((), ()))
_TN = (((0,), (0,)), ((), ()))


def _cparams(sem):
    return pltpu.CompilerParams(dimension_semantics=sem, vmem_limit_bytes=VMEM_LIMIT)


def _inproj_body(x_ref, w_ref, ws_ref, wst_ref, brow_ref, bcol_ref, o_ref, g_ref, gt_ref, xb_ref):
    @pl.when(pl.program_id(1) == 0)
    def _():
        xb = x_ref[...].astype(BF16)
        xb_ref[...] = xb
        g_ref[...] = jnp.dot(xb, ws_ref[...], preferred_element_type=F32) + brow_ref[...]
        gt_ref[...] = lax.dot_general(wst_ref[...], xb, _NT, preferred_element_type=F32) + bcol_ref[...]

    o_ref[...] = jnp.dot(xb_ref[...], w_ref[...], preferred_element_type=F32).astype(BF16)


def _inproj(x, wp, ws, wst, brow, bcol, tm, tn):
    T = x.shape[0]
    return pl.pallas_call(
        _inproj_body,
        grid=(T // tm, N_MAIN // tn),
        in_specs=[
            pl.BlockSpec((tm, D_MODEL), lambda i, j: (i, 0)),
            pl.BlockSpec((D_MODEL, tn), lambda i, j: (0, j)),
            pl.BlockSpec((D_MODEL, LANES), lambda i, j: (0, 0)),
            pl.BlockSpec((LANES, D_MODEL), lambda i, j: (0, 0)),
            pl.BlockSpec((1, LANES), lambda i, j: (0, 0)),
            pl.BlockSpec((LANES, 1), lambda i, j: (0, 0)),
        ],
        out_specs=[
            pl.BlockSpec((tm, tn), lambda i, j: (i, j)),
            pl.BlockSpec((tm, LANES), lambda i, j: (i, 0)),
            pl.BlockSpec((LANES, tm), lambda i, j: (0, i)),
        ],
        out_shape=[
            jax.ShapeDtypeStruct((T, N_MAIN), BF16),
            jax.ShapeDtypeStruct((T, LANES), F32),
            jax.ShapeDtypeStruct((LANES, T), F32),
        ],
        scratch_shapes=[pltpu.VMEM((tm, D_MODEL), BF16)],
        compiler_params=_cparams(("arbitrary", "arbitrary")),
        name="inproj",
    )(x, wp, ws, wst, brow, bcol)


def _kv_body(c_ref, gn_ref, w_ref, o_ref):
    c = c_ref[...].astype(F32)
    r = c * lax.rsqrt(jnp.mean(c * c, axis=-1, keepdims=True) + LN_EPS) * gn_ref[...]
    o_ref[...] = jnp.dot(r.astype(BF16), w_ref[...], preferred_element_type=F32).astype(BF16)


def _kv(proj, gn, wukv, tm):
    T = proj.shape[0]
    return pl.pallas_call(
        _kv_body,
        grid=(T // tm,),
        in_specs=[
            pl.BlockSpec((tm, A_KV_RANK), lambda i: (i, C_CKV // A_KV_RANK)),
            pl.BlockSpec((1, A_KV_RANK), lambda i: (0, 0)),
            pl.BlockSpec((A_KV_RANK, 2 * A_HEAD_DIM), lambda i: (0, 0)),
        ],
        out_specs=pl.BlockSpec((tm, 2 * A_HEAD_DIM), lambda i: (i, 0)),
        out_shape=jax.ShapeDtypeStruct((T, 2 * A_HEAD_DIM), BF16),
        compiler_params=_cparams(("arbitrary",)),
        name="kv",
    )(proj, gn, wukv)


def _attn_body(q_ref, iq_ref, ika_ref, ikb_ref, kv_ref, g_ref, o_ref, key_ref, mb_ref, cand_ref, ans_ref, m_ref,
               acc_ref, *, tq, n_keep):
    qi = pl.program_id(1)
    nck = qi + 1
    t0 = qi * tq
    row_t = t0 + lax.broadcasted_iota(I32, (tq, tq), 0)
    col_l = lax.broadcasted_iota(I32, (tq, tq), 1)

    wq = g_ref[:, G_IW:G_IW + IDX_HEADS] * (IDX_HEADS ** -0.5 * IDX_DIM ** -0.5)

    def score_chunk(c, carry):
        k0 = pl.multiple_of(c * tq, tq)
        ka = ika_ref[pl.ds(k0, tq), :]
        kb = ikb_ref[pl.ds(k0, tq), :]
        acc = jnp.zeros((tq, tq), F32)
        for p in range(IDX_HEADS // 2):
            pair = iq_ref[:, p * LANES:(p + 1) * LANES]
            xe = lax.dot_general(pair, ka, _NT, preferred_element_type=F32)
            xo = lax.dot_general(pair, kb, _NT, preferred_element_type=F32)
            acc = acc + wq[:, 2 * p:2 * p + 1] * jnp.maximum(xe, 0.0)
            acc = acc + wq[:, 2 * p + 1:2 * p + 2] * jnp.maximum(xo, 0.0)
        bits = lax.bitcast_convert_type(acc, I32)
        key = bits ^ ((bits >> 31) & 0x7FFFFFFF)
        key_ref[c] = jnp.where(k0 + col_l <= row_t, key, INT_MIN)
        return carry

    lax.fori_loop(0, nck, score_chunk, 0)

    rows = min(tq, LANES)

    ones_sq = jnp.ones((LANES, LANES), BF16)

    def try_candidate():
        cnts = []
        for r in range(tq // rows):
            rs = slice(r * rows, (r + 1) * rows)

            def count_chunk(c, cnt, rs=rs):
                cb = cand_ref[rs, :]
                k = key_ref[c, rs, :]
                for v in range(tq // LANES):
                    cnt = cnt + jnp.where(k[:, v * LANES:(v + 1) * LANES] >= cb, 1.0, 0.0)
                return cnt

            cnts.append(lax.fori_loop(0, nck, count_chunk, jnp.zeros((rows, LANES), F32)))
        tot = jnp.dot(jnp.concatenate(cnts, axis=0).astype(BF16), ones_sq, preferred_element_type=F32)
        ans_ref[...] = jnp.where(tot >= n_keep, cand_ref[...], ans_ref[...])

    ans_ref[...] = jnp.full((tq, LANES), INT_MIN, I32)
    cand_ref[...] = jnp.zeros((tq, LANES), I32)
    try_candidate()

    def bisect(b, carry):
        cand_ref[...] = ans_ref[...] | lax.shift_left(jnp.int32(1), 30 - b)
        try_candidate()
        return carry

    lax.fori_loop(0, 31, bisect, 0)
    tau = jnp.where(row_t[:, 0:LANES] < n_keep, INT_MIN + 1, ans_ref[...])
    tau = jnp.concatenate([tau] * (tq // LANES), axis=1)

    def mask_chunk(c, carry):
        mb_ref[c] = jnp.where(key_ref[c] >= tau, 0.0, NEG_BIG)
        return carry

    lax.fori_loop(0, nck, mask_chunk, 0)

    scale = A_HEAD_DIM ** -0.5
    c_exp = scale * LOG2_E
    m_ref[...] = jnp.full(m_ref.shape, NEG_BIG, F32)
    acc_ref[...] = jnp.zeros(acc_ref.shape, F32)
    ones = jnp.ones((tq, A_HEAD_DIM), BF16)

    def att_chunk(c, carry):
        k0 = pl.multiple_of(c * tq, tq)
        kc = kv_ref[pl.ds(k0, tq), 0:A_HEAD_DIM]
        v1 = jnp.concatenate([kv_ref[pl.ds(k0, tq), A_HEAD_DIM:2 * A_HEAD_DIM], ones], axis=1)
        rel = (k0 - t0 + lax.broadcasted_iota(I32, (1, tq), 1)).astype(F32)
        for h in range(A_HEADS):
            srow = (2.0 ** (-8.0 * (h + 1) / A_HEADS) / scale) * rel
            for r in range(tq // rows):
                rs = slice(r * rows, (r + 1) * rows)
                qh = q_ref[rs, h * A_HEAD_DIM:(h + 1) * A_HEAD_DIM]
                u = lax.dot_general(qh, kc, _NT, preferred_element_type=F32) + mb_ref[c, rs, :] + srow
                um = u[:, 0:LANES]
                for v in range(1, tq // LANES):
                    um = jnp.maximum(um, u[:, v * LANES:(v + 1) * LANES])
                m = m_ref[h, rs, :]
                m_new = jnp.maximum(m, jnp.broadcast_to(jnp.max(um, axis=-1, keepdims=True), (rows, LANES)))
                m_ref[h, rs, :] = m_new
                alpha = jnp.exp2((m - m_new) * c_exp)
                p = jnp.concatenate([jnp.exp2((u[:, v * LANES:(v + 1) * LANES] - m_new) * c_exp)
                                     for v in range(tq // LANES)], axis=1)
                pv = jnp.dot(p.astype(BF16), v1, preferred_element_type=F32)
                acc = acc_ref[h, rs, :]
                acc_ref[h, rs, :] = jnp.concatenate(
                    [alpha * acc[:, v * LANES:(v + 1) * LANES] for v in range(2 * A_HEAD_DIM // LANES)], axis=1) + pv
        return carry

    lax.fori_loop(0, nck, att_chunk, 0)
    for h in range(A_HEADS):
        num = acc_ref[h, :, 0:A_HEAD_DIM]
        den = acc_ref[h, :, A_HEAD_DIM:2 * A_HEAD_DIM]
        o_ref[:, h * A_HEAD_DIM:(h + 1) * A_HEAD_DIM] = (num / den).astype(BF16)


def _attn(proj, kv, g, B, S, tq):
    T = proj.shape[0]
    nq = S // tq
    n_keep = min(TOPK_MAX, S // 4)
    qw = A_HEADS * A_HEAD_DIM
    iw = IDX_HEADS * IDX_DIM
    return pl.pallas_call(
        functools.partial(_attn_body, tq=tq, n_keep=n_keep),
        grid=(B, nq),
        in_specs=[
            pl.BlockSpec((tq, qw), lambda b, i: (b * nq + i, C_AQ // qw)),
            pl.BlockSpec((tq, iw), lambda b, i: (b * nq + i, C_IQ // iw)),
            pl.BlockSpec((S, LANES), lambda b, i: (b, C_IKA // LANES)),
            pl.BlockSpec((S, LANES), lambda b, i: (b, C_IKB // LANES)),
            pl.BlockSpec((S, 2 * A_HEAD_DIM), lambda b, i: (b, 0)),
            pl.BlockSpec((tq, LANES), lambda b, i: (b * nq + i, 0)),
        ],
        out_specs=pl.BlockSpec((tq, qw), lambda b, i: (b * nq + i, 0)),
        out_shape=jax.ShapeDtypeStruct((T, qw), BF16),
        scratch_shapes=[
            pltpu.VMEM((nq, tq, tq), I32),
            pltpu.VMEM((nq, tq, tq), F32),
            pltpu.VMEM((tq, LANES), I32),
            pltpu.VMEM((tq, LANES), I32),
            pltpu.VMEM((A_HEADS, tq, LANES), F32),
            pltpu.VMEM((A_HEADS, tq, 2 * A_HEAD_DIM), F32),
        ],
        compiler_params=_cparams(("arbitrary", "arbitrary")),
        name="dsa_attn",
    )(proj, proj, proj, proj, kv, g)


def _mlstm_body(q_ref, k_ref, v_ref, o_ref, gt_ref, y_ref, c_ref, n_ref, m_ref, *, L):
    H = M_HEADS

    @pl.when(pl.program_id(1) == 0)
    def _():
        c_ref[...] = jnp.zeros_like(c_ref)
        n_ref[...] = jnp.zeros_like(n_ref)
        m_ref[...] = jnp.zeros_like(m_ref)

    li = gt_ref[G_MI:G_MI + H, :]
    gf = gt_ref[G_MF:G_MF + H, :]
    lf = jnp.minimum(gf, 0.0) - jnp.log1p(jnp.exp(-jnp.abs(gf)))
    lane = lax.broadcasted_iota(I32, (H, L), 1)
    bcum = lf
    sh = 1
    while sh < L:
        bcum = bcum + jnp.where(lane >= sh, pltpu.roll(bcum, sh, axis=1), 0.0)
        sh *= 2
    b_last = bcum[:, L - 1:L]
    w_end = b_last - bcum + li
    a_end = jnp.max(w_end, axis=-1, keepdims=True)
    wexp = jnp.exp(w_end - a_end)
    m_old = m_ref[:, 0:1]
    m_new = jnp.maximum(b_last + m_old, a_end)
    decay = jnp.exp(b_last + m_old - m_new)
    scl = jnp.exp(a_end - m_new)
    inter_log = bcum + m_old
    crow = li - bcum
    m_ref[...] = jnp.broadcast_to(m_new, m_ref.shape)

    stack = jnp.concatenate([bcum, wexp, inter_log, jnp.zeros((LANES - 3 * H, L), F32)], axis=0)
    cols = stack.T

    tri = lax.broadcasted_iota(I32, (L, L), 1) <= lax.broadcasted_iota(I32, (L, L), 0)
    sc = M_QK_DIM ** -0.5
    for h in range(H):
        qh = q_ref[:, h * M_QK_DIM:(h + 1) * M_QK_DIM]
        kh = k_ref[:, h * M_QK_DIM:(h + 1) * M_QK_DIM]
        vh = v_ref[:, h * M_V_DIM:(h + 1) * M_V_DIM]
        oh = o_ref[:, h * M_V_DIM:(h + 1) * M_V_DIM]
        bcum_c = cols[:, h:h + 1]
        wexp_c = cols[:, H + h:H + h + 1]
        ilog_c = cols[:, 2 * H + h:2 * H + h + 1]
        C = c_ref[h]
        nrow = n_ref[h:h + 1, :]

        inter_num = jnp.dot(qh, C.astype(BF16), preferred_element_type=F32) * sc
        inter_den = jnp.sum(qh.astype(F32) * nrow, axis=-1, keepdims=True) * sc
        qk = lax.dot_general(qh, kh, _NT, preferred_element_type=F32) * sc
        dmat = jnp.where(tri, bcum_c + crow[h:h + 1, :], -jnp.inf)
        m_row = jnp.maximum(ilog_c, jnp.max(dmat, axis=-1, keepdims=True))
        inter_w = jnp.exp(ilog_c - m_row)
        intra_w = jnp.exp(dmat - m_row) * qk
        num = inter_w * inter_num + jnp.dot(intra_w.astype(BF16), vh, preferred_element_type=F32)
        den = inter_w * inter_den + jnp.sum(intra_w, axis=-1, keepdims=True)
        hout = num / jnp.maximum(jnp.abs(den), jnp.exp(-m_row))
        y_ref[:, h * M_V_DIM:(h + 1) * M_V_DIM] = (hout * jax.nn.sigmoid(oh.astype(F32))).astype(BF16)

        kw = kh.astype(F32) * wexp_c
        d_h = decay[h:h + 1, :]
        s_h = scl[h:h + 1, :]
        upd = lax.dot_general(kw.astype(BF16), vh, _TN, preferred_element_type=F32)
        c_ref[h] = d_h * C + s_h * upd
        n_ref[h:h + 1, :] = d_h * nrow + s_h * jnp.sum(kw, axis=0, keepdims=True)


def _mlstm(proj, gt, B, S, L):
    T = proj.shape[0]
    nc = S // L
    qw = M_HEADS * M_QK_DIM
    vw = M_HEADS * M_V_DIM
    return pl.pallas_call(
        functools.partial(_mlstm_body, L=L),
        grid=(B, nc),
        in_specs=[
            pl.BlockSpec((L, qw), lambda b, c: (b * nc + c, C_MQ // qw)),
            pl.BlockSpec((L, qw), lambda b, c: (b * nc + c, C_MK // qw)),
            pl.BlockSpec((L, vw), lambda b, c: (b * nc + c, C_MV // vw)),
            pl.BlockSpec((L, vw), lambda b, c: (b * nc + c, C_MO // vw)),
            pl.BlockSpec((LANES, L), lambda b, c: (0, b * nc + c)),
        ],
        out_specs=pl.BlockSpec((L, vw), lambda b, c: (b * nc + c, 0)),
        out_shape=jax.ShapeDtypeStruct((T, vw), BF16),
        scratch_shapes=[
            pltpu.VMEM((M_HEADS, M_QK_DIM, M_V_DIM), F32),
            pltpu.VMEM((M_HEADS, M_QK_DIM), F32),
            pltpu.VMEM((M_HEADS, LANES), F32),
        ],
        compiler_params=_cparams(("arbitrary", "arbitrary")),
        name="mlstm",
    )(proj, proj, proj, proj, gt)


def _mix_body(ya_ref, ym_ref, wa_ref, wm_ref, ga_ref, gm_ref, o_ref):
    a = jnp.dot(ya_ref[...], wa_ref[...], preferred_element_type=F32)
    m = jnp.dot(ym_ref[...], wm_ref[...], preferred_element_type=F32)
    mix = jax.nn.sigmoid(ga_ref[...].astype(F32)) * a + jax.nn.sigmoid(gm_ref[...].astype(F32)) * m
    o_ref[...] = mix.astype(BF16)


def _mix(ya, ym, wa, wm, proj, tm, tn):
    T = ya.shape[0]
    return pl.pallas_call(
        _mix_body,
        grid=(D_MODEL // tn, T // tm),
        in_specs=[
            pl.BlockSpec((tm, ya.shape[1]), lambda j, i: (i, 0)),
            pl.BlockSpec((tm, ym.shape[1]), lambda j, i: (i, 0)),
            pl.BlockSpec((wa.shape[0], tn), lambda j, i: (0, j)),
            pl.BlockSpec((wm.shape[0], tn), lambda j, i: (0, j)),
            pl.BlockSpec((tm, tn), lambda j, i: (i, C_GA // tn + j)),
            pl.BlockSpec((tm, tn), lambda j, i: (i, C_GM // tn + j)),
        ],
        out_specs=pl.BlockSpec((tm, tn), lambda j, i: (i, j)),
        out_shape=jax.ShapeDtypeStruct((T, D_MODEL), BF16),
        compiler_params=_cparams(("arbitrary", "arbitrary")),
        name="mix",
    )(ya, ym, wa, wm, proj, proj)


def _layer_norm(z, g, b):
    mu = jnp.mean(z, axis=-1, keepdims=True)
    d = z - mu
    var = jnp.mean(d * d, axis=-1, keepdims=True)
    return d * lax.rsqrt(var + LN_EPS) * g + b


def _first_max_index(vals):
    vmax = vals[0]
    for v in vals[1:]:
        vmax = jnp.maximum(vmax, v)
    idx = jnp.full(vmax.shape, len(vals) - 1, I32)
    for k in range(len(vals) - 2, -1, -1):
        idx = jnp.where(vals[k] == vmax, k, idx)
    return idx, vmax


def _ln1_body(mix_ref, h_ref, wo_ref, g_ref, b_ref, wrt_ref, brt_ref, h1_ref, route_ref, rt_ref, *, tm):
    z = DN_ALPHA * h_ref[...] + jnp.dot(mix_ref[...], wo_ref[...], preferred_element_type=F32)
    h1 = _layer_norm(z, g_ref[...], b_ref[...])
    h1_ref[...] = h1

    r = lax.dot_general(wrt_ref[...], h1.astype(BF16), _NT, preferred_element_type=F32) + brt_ref[...]
    gl = [r[k:k + 1, :] for k in range(N_GROUPS)]
    g_idx, g_max = _first_max_index(gl)
    g_sum = gl[0] * 0.0
    for v in gl:
        g_sum = g_sum + jnp.exp(v - g_max)
    g_p = 1.0 / g_sum
    el = []
    for e in range(EXPERTS_PER_GROUP):
        rows_e = [N_GROUPS + g * EXPERTS_PER_GROUP + e for g in range(N_GROUPS)]
        v = r[rows_e[-1]:rows_e[-1] + 1, :]
        for g in range(N_GROUPS - 2, -1, -1):
            v = jnp.where(g_idx == g, r[rows_e[g]:rows_e[g] + 1, :], v)
        el.append(v)
    _, e_max = _first_max_index(el)
    ex = [jnp.exp(v - e_max) for v in el]
    e_sum = ex[0] + ex[1] + ex[2] + ex[3]
    ep = [v / e_sum for v in ex]
    i1, p1 = _first_max_index(ep)
    rest = [jnp.where(i1 == e, -1.0, ep[e]) for e in range(EXPERTS_PER_GROUP)]
    i2, p2 = _first_max_index(rest)
    den = p1 + p2
    rows = []
    for e in range(EXPERTS_PER_GROUP):
        w = jnp.where(i1 == e, p1 / den, jnp.where(i2 == e, p2 / den, 0.0))
        rows.append(g_p * w)
    rows.append(g_idx.astype(F32))
    rows.append(jnp.zeros((8 - len(rows), tm), F32))
    route_t = jnp.concatenate(rows, axis=0)
    rt_ref[...] = route_t
    full = jnp.concatenate([route_t, jnp.zeros((LANES - 8, tm), F32)], axis=0)
    route_ref[...] = full.T


def _ln1(mix, h, wo, g, b, wrt, brt, tm):
    T = mix.shape[0]
    return pl.pallas_call(
        functools.partial(_ln1_body, tm=tm),
        grid=(T // tm,),
        in_specs=[
            pl.BlockSpec((tm, D_MODEL), lambda i: (i, 0)),
            pl.BlockSpec((tm, D_MODEL), lambda i: (i, 0)),
            pl.BlockSpec((D_MODEL, D_MODEL), lambda i: (0, 0)),
            pl.BlockSpec((1, D_MODEL), lambda i: (0, 0)),
            pl.BlockSpec((1, D_MODEL), lambda i: (0, 0)),
            pl.BlockSpec((LANES, D_MODEL), lambda i: (0, 0)),
            pl.BlockSpec((LANES, 1), lambda i: (0, 0)),
        ],
        out_specs=[
            pl.BlockSpec((tm, D_MODEL), lambda i: (i, 0)),
            pl.BlockSpec((tm, LANES), lambda i: (i, 0)),
            pl.BlockSpec((8, tm), lambda i: (0, i)),
        ],
        out_shape=[
            jax.ShapeDtypeStruct((T, D_MODEL), F32),
            jax.ShapeDtypeStruct((T, LANES), F32),
            jax.ShapeDtypeStruct((8, T), F32),
        ],
        compiler_params=_cparams(("arbitrary",)),
        name="ln1_router",
    )(mix, h, wo, g, b, wrt, brt)


META_START, META_COUNT, META_TOTAL = 0, N_GROUPS, 2 * N_GROUPS


def _pos_body(gid_ref, pos_ref, meta_ref, *, tile, R):
    gid = gid_ref[...]
    upper = (lax.broadcasted_iota(I32, (LANES, LANES), 0) <= lax.broadcasted_iota(I32, (LANES, LANES), 1))
    upper = jnp.where(upper, 1.0, 0.0).astype(BF16)
    below = lax.broadcasted_iota(I32, (R, R), 1) < lax.broadcasted_iota(I32, (R, R), 0)
    below = jnp.where(below, 1.0, 0.0).astype(BF16)
    lane = lax.broadcasted_iota(I32, (1, LANES), 1)
    pos = jnp.zeros((R, LANES), F32)
    meta = jnp.zeros((1, LANES), F32)
    start = jnp.zeros((1, 1), F32)
    for g in range(N_GROUPS):
        oh = gid == float(g)
        incl = jnp.dot(jnp.where(oh, 1.0, 0.0).astype(BF16), upper, preferred_element_type=F32)
        rowtot = incl[:, LANES - 1:LANES]
        before = jnp.dot(below, jnp.broadcast_to(rowtot, (R, LANES)).astype(BF16), preferred_element_type=F32)
        cnt = jnp.sum(rowtot, axis=0, keepdims=True)
        pos = jnp.where(oh, start + before + incl - 1.0, pos)
        meta = jnp.where(lane == META_START + g, start, meta)
        meta = jnp.where(lane == META_COUNT + g, cnt, meta)
        start = start + jnp.floor((cnt + (tile - 1)) * (1.0 / tile)) * tile
    meta = jnp.where(lane == META_TOTAL, start, meta)
    pos_ref[...] = pos.astype(I32)
    meta_ref[...] = meta.astype(I32)


def _positions(gid, tile):
    R = gid.shape[0]
    return pl.pallas_call(
        functools.partial(_pos_body, tile=tile, R=R),
        out_shape=[jax.ShapeDtypeStruct((R, LANES), I32), jax.ShapeDtypeStruct((1, LANES), I32)],
        name="positions",
    )(gid)


SUB = D_MODEL // LANES
XSUB = SUB + 8


def _store_token_major(dst_ref, val, n, sub=SUB):
    for s in range(SUB):
        dst_ref[pl.ds(s, n, stride=sub), :] = val[:, s * LANES:(s + 1) * LANES]


def _token_block(ref, t, sub=SUB):
    return ref.at[pl.ds(pl.multiple_of(t * sub, sub), sub), :]


def _scatter_body(pos_ref, meta_ref, h_ref, route_ref, xs_ref, xbuf, zx_ref, xsem, *, nb):
    i = pl.program_id(0)
    base = i * nb

    @pl.when(i == 0)
    def _():
        zx_ref[...] = jnp.zeros_like(zx_ref)
        xbuf[...] = jnp.zeros_like(xbuf)
        for g in range(N_GROUPS + 1):
            if g < N_GROUPS:
                lo = meta_ref[META_START + g] + meta_ref[META_COUNT + g]
                hi = meta_ref[META_START + g + 1] if g + 1 < N_GROUPS else meta_ref[META_TOTAL]
            else:
                lo, hi = meta_ref[META_TOTAL], xs_ref.shape[0] // XSUB

            def zcopy(r):
                return pltpu.make_async_copy(zx_ref, _token_block(xs_ref, r, XSUB), xsem)

            def zstart(r, c):
                zcopy(r).start()
                return c

            def zwait(r, c):
                zcopy(r).wait()
                return c

            lax.fori_loop(lo, hi, zstart, 0)
            lax.fori_loop(lo, hi, zwait, 0)

    _store_token_major(xbuf, h_ref[...], nb, XSUB)
    xbuf[pl.ds(SUB, nb, stride=XSUB), :] = route_ref[...]

    def copy(t):
        return pltpu.make_async_copy(_token_block(xbuf, t, XSUB), _token_block(xs_ref, pos_ref[base + t], XSUB), xsem)

    def start(j, c):
        for k in range(2):
            copy(2 * j + k).start(priority=k)
        return c

    def wait(t, c):
        copy(t).wait()
        return c

    lax.fori_loop(0, nb // 2, start, 0)
    lax.fori_loop(0, nb, wait, 0)


def _scatter_rows(pos, meta, h1, route, n_out, nb):
    T = h1.shape[0]
    return pl.pallas_call(
        functools.partial(_scatter_body, nb=nb),
        grid_spec=pltpu.PrefetchScalarGridSpec(
            num_scalar_prefetch=2,
            grid=(T // nb,),
            in_specs=[
                pl.BlockSpec((nb, D_MODEL), lambda i, pos_ref, meta_ref: (i, 0)),
                pl.BlockSpec((nb, LANES), lambda i, pos_ref, meta_ref: (i, 0)),
            ],
            out_specs=pl.BlockSpec(memory_space=pl.ANY),
            scratch_shapes=[
                pltpu.VMEM((nb * XSUB, LANES), F32),
                pltpu.VMEM((XSUB, LANES), F32),
                pltpu.SemaphoreType.DMA(()),
            ],
        ),
        out_shape=jax.ShapeDtypeStruct((n_out * XSUB, LANES), F32),
        compiler_params=pltpu.CompilerParams(dimension_semantics=("arbitrary",), has_side_effects=True,
                                             vmem_limit_bytes=VMEM_LIMIT),
        name="scatter_rows",
    )(pos, meta, h1, route)


def _tile_group(meta_ref, i, tile):
    row = i * tile
    g = jnp.int32(0)
    for k in range(1, N_GROUPS):
        g = g + (row >= meta_ref[META_START + k]).astype(I32)
    return g


def _moe_body(meta_ref, xs_ref, wg_ref, wu_ref, wd_ref, ys_ref, xb_ref, acc_ref, *, tile):
    i = pl.program_id(0)
    e = pl.program_id(1)
    last_e = EXPERTS_PER_GROUP - 1
    valid = i * tile < meta_ref[META_TOTAL]

    @pl.when(jnp.logical_and(valid, e == 0))
    def _():
        for s in range(SUB):
            xb_ref[:, s * LANES:(s + 1) * LANES] = xs_ref[pl.ds(s, tile, stride=XSUB), :].astype(BF16)

    @pl.when(jnp.logical_and(jnp.logical_not(valid), e == 0))
    def _():
        ys_ref[...] = jnp.zeros_like(ys_ref)

    @pl.when(valid)
    def _():
        x = xb_ref[...]
        hg = jnp.dot(x, wg_ref[0], preferred_element_type=F32)
        hu = jnp.dot(x, wu_ref[0], preferred_element_type=F32)
        route = xs_ref[pl.ds(SUB, tile, stride=XSUB), :]
        lane = lax.broadcasted_iota(I32, route.shape, 1)
        cw = jnp.sum(jnp.where(lane == e, route, 0.0), axis=-1, keepdims=True)
        hmid = (hg * jax.nn.sigmoid(hg)) * hu * cw
        y = jnp.dot(hmid.astype(BF16), wd_ref[0], preferred_element_type=F32)

        @pl.when(e == 0)
        def _():
            acc_ref[...] = y

        @pl.when(jnp.logical_and(e > 0, e < last_e))
        def _():
            acc_ref[...] += y

        @pl.when(e == last_e)
        def _():
            _store_token_major(ys_ref, acc_ref[...] + y, tile)


def _moe(meta, xs, wg, wu, wd, tile):
    n_rows = xs.shape[0] // XSUB
    nt = n_rows // tile

    def row_map(i, e, meta_ref):
        last = jnp.maximum(meta_ref[META_TOTAL] // tile - 1, 0)
        return (jnp.minimum(i, last), 0)

    def w_map(i, e, meta_ref):
        last = jnp.maximum(meta_ref[META_TOTAL] // tile - 1, 0)
        g = _tile_group(meta_ref, jnp.minimum(i, last), tile)
        return (g * EXPERTS_PER_GROUP + e, 0, 0)

    return pl.pallas_call(
        functools.partial(_moe_body, tile=tile),
        grid_spec=pltpu.PrefetchScalarGridSpec(
            num_scalar_prefetch=1,
            grid=(nt, EXPERTS_PER_GROUP),
            in_specs=[
                pl.BlockSpec((tile * XSUB, LANES), row_map),
                pl.BlockSpec((1, D_MODEL, EXPERT_FF), w_map),
                pl.BlockSpec((1, D_MODEL, EXPERT_FF), w_map),
                pl.BlockSpec((1, EXPERT_FF, D_MODEL), w_map),
            ],
            out_specs=pl.BlockSpec((tile * SUB, LANES), lambda i, e, meta_ref: (i, 0)),
            scratch_shapes=[pltpu.VMEM((tile, D_MODEL), BF16), pltpu.VMEM((tile, D_MODEL), F32)],
        ),
        out_shape=jax.ShapeDtypeStruct((n_rows * SUB, LANES), F32),
        compiler_params=_cparams(("arbitrary", "arbitrary")),
        name="moe",
    )(meta, xs, wg, wu, wd)


def _ln2_body(pos_ref, h_ref, ys_ref, p_ref, wp_ref, wpg_ref, g_ref, b_ref, o_ref, fbuf, sem, *, tm):
    i = pl.program_id(0)
    slot = i % 2

    def copy(step, t, sl):
        return pltpu.make_async_copy(_token_block(ys_ref, pos_ref[step * tm + t]), _token_block(fbuf.at[sl], t),
                                     sem.at[sl])

    def issue(step, sl):
        def body(j, c):
            for k in range(2):
                copy(step, 2 * j + k, sl).start(priority=k)
            return c

        lax.fori_loop(0, tm // 2, body, 0)

    @pl.when(i == 0)
    def _():
        issue(0, 0)

    @pl.when(i + 1 < pl.num_programs(0))
    def _():
        issue(i + 1, 1 - slot)

    h1 = h_ref[...]
    ple = jnp.dot(p_ref[...].astype(BF16), wp_ref[...], preferred_element_type=F32)
    gate = jnp.dot(h1.astype(BF16), wpg_ref[...], preferred_element_type=F32)
    z = DN_ALPHA * h1 + ple * jax.nn.sigmoid(gate)

    def wait(t, c):
        copy(i, t, slot).wait()
        return c

    lax.fori_loop(0, tm, wait, 0)
    ff = jnp.concatenate([fbuf[slot, pl.ds(s, tm, stride=SUB), :] for s in range(SUB)], axis=1)
    o_ref[...] = _layer_norm(z + ff, g_ref[...], b_ref[...])


def _ln2(pos, h1, ys, p, wple, wpg, g, b, tm):
    T = h1.shape[0]
    return pl.pallas_call(
        functools.partial(_ln2_body, tm=tm),
        grid_spec=pltpu.PrefetchScalarGridSpec(
            num_scalar_prefetch=1,
            grid=(T // tm,),
            in_specs=[
                pl.BlockSpec((tm, D_MODEL), lambda i, pos_ref: (i, 0)),
                pl.BlockSpec(memory_space=pl.ANY),
                pl.BlockSpec((tm, PLE_DIM), lambda i, pos_ref: (i, 0)),
                pl.BlockSpec((PLE_DIM, D_MODEL), lambda i, pos_ref: (0, 0)),
                pl.BlockSpec((D_MODEL, D_MODEL), lambda i, pos_ref: (0, 0)),
                pl.BlockSpec((1, D_MODEL), lambda i, pos_ref: (0, 0)),
                pl.BlockSpec((1, D_MODEL), lambda i, pos_ref: (0, 0)),
            ],
            out_specs=pl.BlockSpec((tm, D_MODEL), lambda i, pos_ref: (i, 0)),
            scratch_shapes=[pltpu.VMEM((2, tm * SUB, LANES), F32), pltpu.SemaphoreType.DMA((2,))],
        ),
        out_shape=jax.ShapeDtypeStruct((T, D_MODEL), F32),
        compiler_params=_cparams(("arbitrary",)),
        name="ple_ln2",
    )(pos, h1, ys, p, wple, wpg, g, b)


def _pack_in_weights(w_in_l, m_i_bias_l, m_f_bias_l):
    parts, off = {}, 0
    for name, n in _SPLITS:
        parts[name] = w_in_l[:, off:off + n]
        off += n
    z64 = jnp.zeros((D_MODEL, IDX_DIM), w_in_l.dtype)
    wp = jnp.concatenate(
        [parts["a_q"], parts["m_q"], parts["m_k"], parts["m_v"], parts["m_o"], parts["g_a"], parts["g_m"],
         parts["i_q"], parts["a_ckv"], parts["i_k"], z64, z64, parts["i_k"]], axis=1).astype(BF16)
    pad = jnp.zeros((D_MODEL, LANES - IDX_HEADS - 2 * M_HEADS), w_in_l.dtype)
    ws = jnp.concatenate([parts["i_w"], parts["m_i"], parts["m_f"], pad], axis=1).astype(BF16)
    bias = jnp.concatenate([jnp.zeros((IDX_HEADS,), F32), m_i_bias_l.astype(F32), m_f_bias_l.astype(F32),
                            jnp.zeros((LANES - IDX_HEADS - 2 * M_HEADS,), F32)])
    return wp, ws, ws.T, bias.reshape(1, LANES), bias.reshape(LANES, 1)


def _tiles(T, S):
    return dict(
        inproj_tm=min(1024, T), inproj_tn=1536, row_tm=min(512, T), mix_tn=1024,
        attn_tq=min(256, S), mlstm_l=min(256, S), moe_tile=min(512, T // 4), dma_rows=min(256, T),
    )


def kernel(x, p, w_in, kv_norm, w_ukv, m_i_bias, m_f_bias, w_proj_a, w_proj_m, w_out, ln1_g, ln1_b, w_grp, b_grp,
           w_exp, b_exp, w_gate, w_up, w_down, w_ple, w_ple_gate, ln2_g, ln2_b):
    B, S, D = x.shape
    T = B * S
    assert D == D_MODEL and T % LANES == 0
    cfg = _tiles(T, S)
    h = x.reshape(T, D)
    for l in range(w_in.shape[0]):
        wp, ws, wst, brow, bcol = _pack_in_weights(w_in[l], m_i_bias[l], m_f_bias[l])
        proj, g, gt = _inproj(h, wp, ws, wst, brow, bcol, cfg["inproj_tm"], cfg["inproj_tn"])
        kv = _kv(proj, kv_norm[l].reshape(1, -1).astype(F32), w_ukv[l].astype(BF16), cfg["row_tm"])
        y_a = _attn(proj, kv, g, B, S, cfg["attn_tq"])
        y_m = _mlstm(proj, gt, B, S, cfg["mlstm_l"])
        mix = _mix(y_a, y_m, w_proj_a[l].astype(BF16), w_proj_m[l].astype(BF16), proj, cfg["row_tm"], cfg["mix_tn"])

        n_route = N_GROUPS + N_EXPERTS
        wr = jnp.concatenate([w_grp[l], w_exp[l], jnp.zeros((D, LANES - n_route), F32)], axis=1)
        br = jnp.concatenate([b_grp[l], b_exp[l], jnp.zeros((LANES - n_route,), F32)])
        h1, route, rt = _ln1(mix, h, w_out[l].astype(BF16), ln1_g[l].reshape(1, D), ln1_b[l].reshape(1, D),
                             wr.T.astype(BF16), br.reshape(LANES, 1), cfg["row_tm"])

        tile = cfg["moe_tile"]
        pos, meta = _positions(rt[N_GROUPS].reshape(T // LANES, LANES), tile)
        pos = pos.reshape(T)
        meta = meta.reshape(LANES)
        xs = _scatter_rows(pos, meta, h1, route, T + N_GROUPS * tile, cfg["dma_rows"])
        ys = _moe(meta, xs, w_gate[l].astype(BF16), w_up[l].astype(BF16), w_down[l].astype(BF16), tile)
        h = _ln2(pos, h1, ys, p[l].reshape(T, PLE_DIM), w_ple[l].astype(BF16), w_ple_gate[l].astype(BF16),
                 ln2_g[l].reshape(1, D), ln2_b[l].reshape(1, D), cfg["row_tm"])
    return h.reshape(B, S, D)
```

```python
import functools

import jax
import jax.numpy as jnp
from jax import lax
from jax.experimental import pallas as pl
from jax.experimental.pallas import tpu as pltpu

F32 = jnp.float32
BF16 = jnp.bfloat16
I32 = jnp.int32

D_MODEL = 2048
DEPTH = 2
A_HEADS = 16
A_HEAD_DIM = 128
A_KV_RANK = 256
IDX_HEADS = 16
IDX_DIM = 64
TOPK_MAX = 256
M_HEADS = 8
M_QK_DIM = 128
M_V_DIM = 256
N_GROUPS = 4
EXPERTS_PER_GROUP = 4
N_EXPERTS = N_GROUPS * EXPERTS_PER_GROUP
EXPERT_FF = 512
PLE_DIM = 256
LN_EPS = 1e-5
DN_ALPHA = (2 * DEPTH) ** 0.25

_SPLITS = (
    ("a_q", A_HEADS * A_HEAD_DIM), ("a_ckv", A_KV_RANK), ("i_q", IDX_HEADS * IDX_DIM), ("i_k", IDX_DIM),
    ("i_w", IDX_HEADS), ("m_q", M_HEADS * M_QK_DIM), ("m_k", M_HEADS * M_QK_DIM), ("m_v", M_HEADS * M_V_DIM),
    ("m_o", M_HEADS * M_V_DIM), ("m_i", M_HEADS), ("m_f", M_HEADS), ("g_a", D_MODEL), ("g_m", D_MODEL),
)

C_AQ, C_MQ, C_MK, C_MV, C_MO, C_GA, C_GM = 0, 2048, 3072, 4096, 6144, 8192, 10240
C_IQ, C_CKV, C_IKA, C_IKB = 12288, 13312, 13568, 13696
N_MAIN = 13824
G_IW, G_MI, G_MF = 0, 16, 24
LANES = 128

INT_MIN = -(2 ** 31)
NEG_BIG = -1e30
LOG2_E = 1.4426950408889634
VMEM_LIMIT = 56 * 1024 * 1024

_NT = (((1,), (1,)), ((), ()))
_TN = (((0,), (0,)), ((), ()))


def _cparams(sem):
    return pltpu.CompilerParams(dimension_semantics=sem, vmem_limit_bytes=VMEM_LIMIT)


def _inproj_body(x_ref, w_ref, ws_ref, wst_ref, brow_ref, bcol_ref, o_ref, g_ref, gt_ref, xb_ref):
    @pl.when(pl.program_id(1) == 0)
    def _():
        xb = x_ref[...].astype(BF16)
        xb_ref[...] = xb
        g_ref[...] = jnp.dot(xb, ws_ref[...], preferred_element_type=F32) + brow_ref[...]
        gt_ref[...] = lax.dot_general(wst_ref[...], xb, _NT, preferred_element_type=F32) + bcol_ref[...]

    o_ref[...] = jnp.dot(xb_ref[...], w_ref[...], preferred_element_type=F32).astype(BF16)


def _inproj(x, wp, ws, wst, brow, bcol, tm, tn):
    T = x.shape[0]
    return pl.pallas_call(
        _inproj_body,
        grid=(T // tm, N_MAIN // tn),
        in_specs=[
            pl.BlockSpec((tm, D_MODEL), lambda i, j: (i, 0)),
            pl.BlockSpec((D_MODEL, tn), lambda i, j: (0, j)),
            pl.BlockSpec((D_MODEL, LANES), lambda i, j: (0, 0)),
            pl.BlockSpec((LANES, D_MODEL), lambda i, j: (0, 0)),
            pl.BlockSpec((1, LANES), lambda i, j: (0, 0)),
            pl.BlockSpec((LANES, 1), lambda i, j: (0, 0)),
        ],
        out_specs=[
            pl.BlockSpec((tm, tn), lambda i, j: (i, j)),
            pl.BlockSpec((tm, LANES), lambda i, j: (i, 0)),
            pl.BlockSpec((LANES, tm), lambda i, j: (0, i)),
        ],
        out_shape=[
            jax.ShapeDtypeStruct((T, N_MAIN), BF16),
            jax.ShapeDtypeStruct((T, LANES), F32),
            jax.ShapeDtypeStruct((LANES, T), F32),
        ],
        scratch_shapes=[pltpu.VMEM((tm, D_MODEL), BF16)],
        compiler_params=_cparams(("arbitrary", "arbitrary")),
        name="inproj",
    )(x, wp, ws, wst, brow, bcol)


def _kv_body(c_ref, gn_ref, w_ref, o_ref):
    c = c_ref[...].astype(F32)
    r = c * lax.rsqrt(jnp.mean(c * c, axis=-1, keepdims=True) + LN_EPS) * gn_ref[...]
    o_ref[...] = jnp.dot(r.astype(BF16), w_ref[...], preferred_element_type=F32).astype(BF16)


def _kv(proj, gn, wukv, tm):
    T = proj.shape[0]
    return pl.pallas_call(
        _kv_body,
        grid=(T // tm,),
        in_specs=[
            pl.BlockSpec((tm, A_KV_RANK), lambda i: (i, C_CKV // A_KV_RANK)),
            pl.BlockSpec((1, A_KV_RANK), lambda i: (0, 0)),
            pl.BlockSpec((A_KV_RANK, 2 * A_HEAD_DIM), lambda i: (0, 0)),
        ],
        out_specs=pl.BlockSpec((tm, 2 * A_HEAD_DIM), lambda i: (i, 0)),
        out_shape=jax.ShapeDtypeStruct((T, 2 * A_HEAD_DIM), BF16),
        compiler_params=_cparams(("arbitrary",)),
        name="kv",
    )(proj, gn, wukv)


def _attn_body(q_ref, iq_ref, ika_ref, ikb_ref, kv_ref, g_ref, o_ref, key_ref, mb_ref, cand_ref, ans_ref, m_ref,
               acc_ref, *, tq, n_keep):
    qi = pl.program_id(1)
    nck = qi + 1
    t0 = qi * tq
    row_t = t0 + lax.broadcasted_iota(I32, (tq, tq), 0)
    col_l = lax.broadcasted_iota(I32, (tq, tq), 1)

    wq = g_ref[:, G_IW:G_IW + IDX_HEADS] * (IDX_HEADS ** -0.5 * IDX_DIM ** -0.5)

    def score_chunk(c, carry):
        k0 = pl.multiple_of(c * tq, tq)
        ka = ika_ref[pl.ds(k0, tq), :]
        kb = ikb_ref[pl.ds(k0, tq), :]
        acc = jnp.zeros((tq, tq), F32)
        for p in range(IDX_HEADS // 2):
            pair = iq_ref[:, p * LANES:(p + 1) * LANES]
            xe = lax.dot_general(pair, ka, _NT, preferred_element_type=F32)
            xo = lax.dot_general(pair, kb, _NT, preferred_element_type=F32)
            acc = acc + wq[:, 2 * p:2 * p + 1] * jnp.maximum(xe, 0.0)
            acc = acc + wq[:, 2 * p + 1:2 * p + 2] * jnp.maximum(xo, 0.0)
        bits = lax.bitcast_convert_type(acc, I32)
        key = bits ^ ((bits >> 31) & 0x7FFFFFFF)
        key_ref[c] = jnp.where(k0 + col_l <= row_t, key, INT_MIN)
        return carry

    lax.fori_loop(0, nck, score_chunk, 0)

    rows = min(tq, LANES)

    ones_sq = jnp.ones((LANES, LANES), BF16)

    def try_candidate():
        cnts = []
        for r in range(tq // rows):
            rs = slice(r * rows, (r + 1) * rows)

            def count_chunk(c, cnt, rs=rs):
                cb = cand_ref[rs, :]
                k = key_ref[c, rs, :]
                for v in range(tq // LANES):
                    cnt = cnt + jnp.where(k[:, v * LANES:(v + 1) * LANES] >= cb, 1.0, 0.0)
                return cnt

            cnts.append(lax.fori_loop(0, nck, count_chunk, jnp.zeros((rows, LANES), F32)))
        tot = jnp.dot(jnp.concatenate(cnts, axis=0).astype(BF16), ones_sq, preferred_element_type=F32)
        ans_ref[...] = jnp.where(tot >= n_keep, cand_ref[...], ans_ref[...])

    ans_ref[...] = jnp.full((tq, LANES), INT_MIN, I32)
    cand_ref[...] = jnp.zeros((tq, LANES), I32)
    try_candidate()

    def bisect(b, carry):
        cand_ref[...] = ans_ref[...] | lax.shift_left(jnp.int32(1), 30 - b)
        try_candidate()
        return carry

    lax.fori_loop(0, 31, bisect, 0)
    tau = jnp.where(row_t[:, 0:LANES] < n_keep, INT_MIN + 1, ans_ref[...])
    tau = jnp.concatenate([tau] * (tq // LANES), axis=1)

    def mask_chunk(c, carry):
        mb_ref[c] = jnp.where(key_ref[c] >= tau, 0.0, NEG_BIG)
        return carry

    lax.fori_loop(0, nck, mask_chunk, 0)

    scale = A_HEAD_DIM ** -0.5
    c_exp = scale * LOG2_E
    m_ref[...] = jnp.full(m_ref.shape, NEG_BIG, F32)
    acc_ref[...] = jnp.zeros(acc_ref.shape, F32)
    ones = jnp.ones((tq, A_HEAD_DIM), BF16)

    def att_chunk(c, carry):
        k0 = pl.multiple_of(c * tq, tq)
        kc = kv_ref[pl.ds(k0, tq), 0:A_HEAD_DIM]
        v1 = jnp.concatenate([kv_ref[pl.ds(k0, tq), A_HEAD_DIM:2 * A_HEAD_DIM], ones], axis=1)
        rel = (k0 - t0 + lax.broadcasted_iota(I32, (1, tq), 1)).astype(F32)
        for h in range(A_HEADS):
            srow = (2.0 ** (-8.0 * (h + 1) / A_HEADS) / scale) * rel
            for r in range(tq // rows):
                rs = slice(r * rows, (r + 1) * rows)
                qh = q_ref[rs, h * A_HEAD_DIM:(h + 1) * A_HEAD_DIM]
                u = lax.dot_general(qh, kc, _NT, preferred_element_type=F32) + mb_ref[c, rs, :] + srow
                um = u[:, 0:LANES]
                for v in range(1, tq // LANES):
                    um = jnp.maximum(um, u[:, v * LANES:(v + 1) * LANES])
                m = m_ref[h, rs, :]
                m_new = jnp.maximum(m, jnp.broadcast_to(jnp.max(um, axis=-1, keepdims=True), (rows, LANES)))
                m_ref[h, rs, :] = m_new
                alpha = jnp.exp2((m - m_new) * c_exp)
                p = jnp.concatenate([jnp.exp2((u[:, v * LANES:(v + 1) * LANES] - m_new) * c_exp)
                                     for v in range(tq // LANES)], axis=1)
                pv = jnp.dot(p.astype(BF16), v1, preferred_element_type=F32)
                acc = acc_ref[h, rs, :]
                acc_ref[h, rs, :] = jnp.concatenate(
                    [alpha * acc[:, v * LANES:(v + 1) * LANES] for v in range(2 * A_HEAD_DIM // LANES)], axis=1) + pv
        return carry

    lax.fori_loop(0, nck, att_chunk, 0)
    for h in range(A_HEADS):
        num = acc_ref[h, :, 0:A_HEAD_DIM]
        den = acc_ref[h, :, A_HEAD_DIM:2 * A_HEAD_DIM]
        o_ref[:, h * A_HEAD_DIM:(h + 1) * A_HEAD_DIM] = (num / den).astype(BF16)


def _attn(proj, kv, g, B, S, tq):
    T = proj.shape[0]
    nq = S // tq
    n_keep = min(TOPK_MAX, S // 4)
    qw = A_HEADS * A_HEAD_DIM
    iw = IDX_HEADS * IDX_DIM
    return pl.pallas_call(
        functools.partial(_attn_body, tq=tq, n_keep=n_keep),
        grid=(B, nq),
        in_specs=[
            pl.BlockSpec((tq, qw), lambda b, i: (b * nq + i, C_AQ // qw)),
            pl.BlockSpec((tq, iw), lambda b, i: (b * nq + i, C_IQ // iw)),
            pl.BlockSpec((S, LANES), lambda b, i: (b, C_IKA // LANES)),
            pl.BlockSpec((S, LANES), lambda b, i: (b, C_IKB // LANES)),
            pl.BlockSpec((S, 2 * A_HEAD_DIM), lambda b, i: (b, 0)),
            pl.BlockSpec((tq, LANES), lambda b, i: (b * nq + i, 0)),
        ],
        out_specs=pl.BlockSpec((tq, qw), lambda b, i: (b * nq + i, 0)),
        out_shape=jax.ShapeDtypeStruct((T, qw), BF16),
        scratch_shapes=[
            pltpu.VMEM((nq, tq, tq), I32),
            pltpu.VMEM((nq, tq, tq), F32),
            pltpu.VMEM((tq, LANES), I32),
            pltpu.VMEM((tq, LANES), I32),
            pltpu.VMEM((A_HEADS, tq, LANES), F32),
            pltpu.VMEM((A_HEADS, tq, 2 * A_HEAD_DIM), F32),
        ],
        compiler_params=_cparams(("arbitrary", "arbitrary")),
        name="dsa_attn",
    )(proj, proj, proj, proj, kv, g)


def _mlstm_body(q_ref, k_ref, v_ref, o_ref, gt_ref, y_ref, c_ref, n_ref, m_ref, *, L):
    H = M_HEADS

    @pl.when(pl.program_id(1) == 0)
    def _():
        c_ref[...] = jnp.zeros_like(c_ref)
        n_ref[...] = jnp.zeros_like(n_ref)
        m_ref[...] = jnp.zeros_like(m_ref)

    li = gt_ref[G_MI:G_MI + H, :]
    gf = gt_ref[G_MF:G_MF + H, :]
    lf = jnp.minimum(gf, 0.0) - jnp.log1p(jnp.exp(-jnp.abs(gf)))
    lane = lax.broadcasted_iota(I32, (H, L), 1)
    bcum = lf
    sh = 1
    while sh < L:
        bcum = bcum + jnp.where(lane >= sh, pltpu.roll(bcum, sh, axis=1), 0.0)
        sh *= 2
    b_last = bcum[:, L - 1:L]
    w_end = b_last - bcum + li
    a_end = jnp.max(w_end, axis=-1, keepdims=True)
    wexp = jnp.exp(w_end - a_end)
    m_old = m_ref[:, 0:1]
    m_new = jnp.maximum(b_last + m_old, a_end)
    decay = jnp.exp(b_last + m_old - m_new)
    scl = jnp.exp(a_end - m_new)
    inter_log = bcum + m_old
    crow = li - bcum
    m_ref[...] = jnp.broadcast_to(m_new, m_ref.shape)

    stack = jnp.concatenate([bcum, wexp, inter_log, jnp.zeros((LANES - 3 * H, L), F32)], axis=0)
    cols = stack.T

    tri = lax.broadcasted_iota(I32, (L, L), 1) <= lax.broadcasted_iota(I32, (L, L), 0)
    sc = M_QK_DIM ** -0.5
    for h in range(H):
        qh = q_ref[:, h * M_QK_DIM:(h + 1) * M_QK_DIM]
        kh = k_ref[:, h * M_QK_DIM:(h + 1) * M_QK_DIM]
        vh = v_ref[:, h * M_V_DIM:(h + 1) * M_V_DIM]
        oh = o_ref[:, h * M_V_DIM:(h + 1) * M_V_DIM]
        bcum_c = cols[:, h:h + 1]
        wexp_c = cols[:, H + h:H + h + 1]
        ilog_c = cols[:, 2 * H + h:2 * H + h + 1]
        C = c_ref[h]
        nrow = n_ref[h:h + 1, :]

        inter_num = jnp.dot(qh, C.astype(BF16), preferred_element_type=F32) * sc
        inter_den = jnp.sum(qh.astype(F32) * nrow, axis=-1, keepdims=True) * sc
        qk = lax.dot_general(qh, kh, _NT, preferred_element_type=F32) * sc
        dmat = jnp.where(tri, bcum_c + crow[h:h + 1, :], -jnp.inf)
        m_row = jnp.maximum(ilog_c, jnp.max(dmat, axis=-1, keepdims=True))
        inter_w = jnp.exp(ilog_c - m_row)
        intra_w = jnp.exp(dmat - m_row) * qk
        num = inter_w * inter_num + jnp.dot(intra_w.astype(BF16), vh, preferred_element_type=F32)
        den = inter_w * inter_den + jnp.sum(intra_w, axis=-1, keepdims=True)
        hout = num / jnp.maximum(jnp.abs(den), jnp.exp(-m_row))
        y_ref[:, h * M_V_DIM:(h + 1) * M_V_DIM] = (hout * jax.nn.sigmoid(oh.astype(F32))).astype(BF16)

        kw = kh.astype(F32) * wexp_c
        d_h = decay[h:h + 1, :]
        s_h = scl[h:h + 1, :]
        upd = lax.dot_general(kw.astype(BF16), vh, _TN, preferred_element_type=F32)
        c_ref[h] = d_h * C + s_h * upd
        n_ref[h:h + 1, :] = d_h * nrow + s_h * jnp.sum(kw, axis=0, keepdims=True)


def _mlstm(proj, gt, B, S, L):
    T = proj.shape[0]
    nc = S // L
    qw = M_HEADS * M_QK_DIM
    vw = M_HEADS * M_V_DIM
    return pl.pallas_call(
        functools.partial(_mlstm_body, L=L),
        grid=(B, nc),
        in_specs=[
            pl.BlockSpec((L, qw), lambda b, c: (b * nc + c, C_MQ // qw)),
            pl.BlockSpec((L, qw), lambda b, c: (b * nc + c, C_MK // qw)),
            pl.BlockSpec((L, vw), lambda b, c: (b * nc + c, C_MV // vw)),
            pl.BlockSpec((L, vw), lambda b, c: (b * nc + c, C_MO // vw)),
            pl.BlockSpec((LANES, L), lambda b, c: (0, b * nc + c)),
        ],
        out_specs=pl.BlockSpec((L, vw), lambda b, c: (b * nc + c, 0)),
        out_shape=jax.ShapeDtypeStruct((T, vw), BF16),
        scratch_shapes=[
            pltpu.VMEM((M_HEADS, M_QK_DIM, M_V_DIM), F32),
            pltpu.VMEM((M_HEADS, M_QK_DIM), F32),
            pltpu.VMEM((M_HEADS, LANES), F32),
        ],
        compiler_params=_cparams(("arbitrary", "arbitrary")),
        name="mlstm",
    )(proj, proj, proj, proj, gt)


def _mix_body(ya_ref, ym_ref, wa_ref, wm_ref, ga_ref, gm_ref, o_ref):
    a = jnp.dot(ya_ref[...], wa_ref[...], preferred_element_type=F32)
    m = jnp.dot(ym_ref[...], wm_ref[...], preferred_element_type=F32)
    mix = jax.nn.sigmoid(ga_ref[...].astype(F32)) * a + jax.nn.sigmoid(gm_ref[...].astype(F32)) * m
    o_ref[...] = mix.astype(BF16)


def _mix(ya, ym, wa, wm, proj, tm, tn):
    T = ya.shape[0]
    return pl.pallas_call(
        _mix_body,
        grid=(D_MODEL // tn, T // tm),
        in_specs=[
            pl.BlockSpec((tm, ya.shape[1]), lambda j, i: (i, 0)),
            pl.BlockSpec((tm, ym.shape[1]), lambda j, i: (i, 0)),
            pl.BlockSpec((wa.shape[0], tn), lambda j, i: (0, j)),
            pl.BlockSpec((wm.shape[0], tn), lambda j, i: (0, j)),
            pl.BlockSpec((tm, tn), lambda j, i: (i, C_GA // tn + j)),
            pl.BlockSpec((tm, tn), lambda j, i: (i, C_GM // tn + j)),
        ],
        out_specs=pl.BlockSpec((tm, tn), lambda j, i: (i, j)),
        out_shape=jax.ShapeDtypeStruct((T, D_MODEL), BF16),
        compiler_params=_cparams(("arbitrary", "arbitrary")),
        name="mix",
    )(ya, ym, wa, wm, proj, proj)


def _layer_norm(z, g, b):
    mu = jnp.mean(z, axis=-1, keepdims=True)
    d = z - mu
    var = jnp.mean(d * d, axis=-1, keepdims=True)
    return d * lax.rsqrt(var + LN_EPS) * g + b


def _first_max_index(vals):
    vmax = vals[0]
    for v in vals[1:]:
        vmax = jnp.maximum(vmax, v)
    idx = jnp.full(vmax.shape, len(vals) - 1, I32)
    for k in range(len(vals) - 2, -1, -1):
        idx = jnp.where(vals[k] == vmax, k, idx)
    return idx, vmax


def _ln1_body(mix_ref, h_ref, wo_ref, g_ref, b_ref, wrt_ref, brt_ref, h1_ref, route_ref, rt_ref, *, tm):
    z = DN_ALPHA * h_ref[...] + jnp.dot(mix_ref[...], wo_ref[...], preferred_element_type=F32)
    h1 = _layer_norm(z, g_ref[...], b_ref[...])
    h1_ref[...] = h1

    r = lax.dot_general(wrt_ref[...], h1.astype(BF16), _NT, preferred_element_type=F32) + brt_ref[...]
    gl = [r[k:k + 1, :] for k in range(N_GROUPS)]
    g_idx, g_max = _first_max_index(gl)
    g_sum = gl[0] * 0.0
    for v in gl:
        g_sum = g_sum + jnp.exp(v - g_max)
    g_p = 1.0 / g_sum
    el = []
    for e in range(EXPERTS_PER_GROUP):
        rows_e = [N_GROUPS + g * EXPERTS_PER_GROUP + e for g in range(N_GROUPS)]
        v = r[rows_e[-1]:rows_e[-1] + 1, :]
        for g in range(N_GROUPS - 2, -1, -1):
            v = jnp.where(g_idx == g, r[rows_e[g]:rows_e[g] + 1, :], v)
        el.append(v)
    _, e_max = _first_max_index(el)
    ex = [jnp.exp(v - e_max) for v in el]
    e_sum = ex[0] + ex[1] + ex[2] + ex[3]
    ep = [v / e_sum for v in ex]
    i1, p1 = _first_max_index(ep)
    rest = [jnp.where(i1 == e, -1.0, ep[e]) for e in range(EXPERTS_PER_GROUP)]
    i2, p2 = _first_max_index(rest)
    den = p1 + p2
    rows = []
    for e in range(EXPERTS_PER_GROUP):
        w = jnp.where(i1 == e, p1 / den, jnp.where(i2 == e, p2 / den, 0.0))
        rows.append(g_p * w)
    rows.append(g_idx.astype(F32))
    rows.append(jnp.zeros((8 - len(rows), tm), F32))
    route_t = jnp.concatenate(rows, axis=0)
    rt_ref[...] = route_t
    full = jnp.concatenate([route_t, jnp.zeros((LANES - 8, tm), F32)], axis=0)
    route_ref[...] = full.T


def _ln1(mix, h, wo, g, b, wrt, brt, tm):
    T = mix.shape[0]
    return pl.pallas_call(
        functools.partial(_ln1_body, tm=tm),
        grid=(T // tm,),
        in_specs=[
            pl.BlockSpec((tm, D_MODEL), lambda i: (i, 0)),
            pl.BlockSpec((tm, D_MODEL), lambda i: (i, 0)),
            pl.BlockSpec((D_MODEL, D_MODEL), lambda i: (0, 0)),
            pl.BlockSpec((1, D_MODEL), lambda i: (0, 0)),
            pl.BlockSpec((1, D_MODEL), lambda i: (0, 0)),
            pl.BlockSpec((LANES, D_MODEL), lambda i: (0, 0)),
            pl.BlockSpec((LANES, 1), lambda i: (0, 0)),
        ],
        out_specs=[
            pl.BlockSpec((tm, D_MODEL), lambda i: (i, 0)),
            pl.BlockSpec((tm, LANES), lambda i: (i, 0)),
            pl.BlockSpec((8, tm), lambda i: (0, i)),
        ],
        out_shape=[
            jax.ShapeDtypeStruct((T, D_MODEL), F32),
            jax.ShapeDtypeStruct((T, LANES), F32),
            jax.ShapeDtypeStruct((8, T), F32),
        ],
        compiler_params=_cparams(("arbitrary",)),
        name="ln1_router",
    )(mix, h, wo, g, b, wrt, brt)


META_START, META_COUNT, META_TOTAL = 0, N_GROUPS, 2 * N_GROUPS


def _pos_body(gid_ref, pos_ref, meta_ref, *, tile, R):
    gid = gid_ref[...]
    upper = (lax.broadcasted_iota(I32, (LANES, LANES), 0) <= lax.broadcasted_iota(I32, (LANES, LANES), 1))
    upper = jnp.where(upper, 1.0, 0.0).astype(BF16)
    below = lax.broadcasted_iota(I32, (R, R), 1) < lax.broadcasted_iota(I32, (R, R), 0)
    below = jnp.where(below, 1.0, 0.0).astype(BF16)
    lane = lax.broadcasted_iota(I32, (1, LANES), 1)
    pos = jnp.zeros((R, LANES), F32)
    meta = jnp.zeros((1, LANES), F32)
    start = jnp.zeros((1, 1), F32)
    for g in range(N_GROUPS):
        oh = gid == float(g)
        incl = jnp.dot(jnp.where(oh, 1.0, 0.0).astype(BF16), upper, preferred_element_type=F32)
        rowtot = incl[:, LANES - 1:LANES]
        before = jnp.dot(below, jnp.broadcast_to(rowtot, (R, LANES)).astype(BF16), preferred_element_type=F32)
        cnt = jnp.sum(rowtot, axis=0, keepdims=True)
        pos = jnp.where(oh, start + before + incl - 1.0, pos)
        meta = jnp.where(lane == META_START + g, start, meta)
        meta = jnp.where(lane == META_COUNT + g, cnt, meta)
        start = start + jnp.floor((cnt + (tile - 1)) * (1.0 / tile)) * tile
    meta = jnp.where(lane == META_TOTAL, start, meta)
    pos_ref[...] = pos.astype(I32)
    meta_ref[...] = meta.astype(I32)


def _positions(gid, tile):
    R = gid.shape[0]
    return pl.pallas_call(
        functools.partial(_pos_body, tile=tile, R=R),
        out_shape=[jax.ShapeDtypeStruct((R, LANES), I32), jax.ShapeDtypeStruct((1, LANES), I32)],
        name="positions",
    )(gid)


SUB = D_MODEL // LANES
XSUB = SUB + 8


def _store_token_major(dst_ref, val, n, sub=SUB):
    for s in range(SUB):
        dst_ref[pl.ds(s, n, stride=sub), :] = val[:, s * LANES:(s + 1) * LANES]


def _token_block(ref, t, sub=SUB):
    return ref.at[pl.ds(pl.multiple_of(t * sub, sub), sub), :]


def _scatter_body(pos_ref, meta_ref, h_ref, route_ref, xs_ref, xbuf, zx_ref, xsem, *, nb):
    i = pl.program_id(0)
    base = i * nb

    @pl.when(i == 0)
    def _():
        zx_ref[...] = jnp.zeros_like(zx_ref)
        xbuf[...] = jnp.zeros_like(xbuf)
        for g in range(N_GROUPS + 1):
            if g < N_GROUPS:
                lo = meta_ref[META_START + g] + meta_ref[META_COUNT + g]
                hi = meta_ref[META_START + g + 1] if g + 1 < N_GROUPS else meta_ref[META_TOTAL]
            else:
                lo, hi = meta_ref[META_TOTAL], xs_ref.shape[0] // XSUB

            def zcopy(r):
                return pltpu.make_async_copy(zx_ref, _token_block(xs_ref, r, XSUB), xsem)

            def zstart(r, c):
                zcopy(r).start()
                return c

            def zwait(r, c):
                zcopy(r).wait()
                return c

            lax.fori_loop(lo, hi, zstart, 0)
            lax.fori_loop(lo, hi, zwait, 0)

    _store_token_major(xbuf, h_ref[...], nb, XSUB)
    xbuf[pl.ds(SUB, nb, stride=XSUB), :] = route_ref[...]

    def copy(t):
        return pltpu.make_async_copy(_token_block(xbuf, t, XSUB), _token_block(xs_ref, pos_ref[base + t], XSUB), xsem)

    def start(j, c):
        for k in range(2):
            copy(2 * j + k).start(priority=k)
        return c

    def wait(t, c):
        copy(t).wait()
        return c

    lax.fori_loop(0, nb // 2, start, 0)
    lax.fori_loop(0, nb, wait, 0)


def _scatter_rows(pos, meta, h1, route, n_out, nb):
    T = h1.shape[0]
    return pl.pallas_call(
        functools.partial(_scatter_body, nb=nb),
        grid_spec=pltpu.PrefetchScalarGridSpec(
            num_scalar_prefetch=2,
            grid=(T // nb,),
            in_specs=[
                pl.BlockSpec((nb, D_MODEL), lambda i, pos_ref, meta_ref: (i, 0)),
                pl.BlockSpec((nb, LANES), lambda i, pos_ref, meta_ref: (i, 0)),
            ],
            out_specs=pl.BlockSpec(memory_space=pl.ANY),
            scratch_shapes=[
                pltpu.VMEM((nb * XSUB, LANES), F32),
                pltpu.VMEM((XSUB, LANES), F32),
                pltpu.SemaphoreType.DMA(()),
            ],
        ),
        out_shape=jax.ShapeDtypeStruct((n_out * XSUB, LANES), F32),
        compiler_params=pltpu.CompilerParams(dimension_semantics=("arbitrary",), has_side_effects=True,
                                             vmem_limit_bytes=VMEM_LIMIT),
        name="scatter_rows",
    )(pos, meta, h1, route)


def _tile_group(meta_ref, i, tile):
    row = i * tile
    g = jnp.int32(0)
    for k in range(1, N_GROUPS):
        g = g + (row >= meta_ref[META_START + k]).astype(I32)
    return g


def _moe_body(meta_ref, xs_ref, wg_ref, wu_ref, wd_ref, ys_ref, xb_ref, acc_ref, *, tile):
    i = pl.program_id(0)
    e = pl.program_id(1)
    last_e = EXPERTS_PER_GROUP - 1
    valid = i * tile < meta_ref[META_TOTAL]

    @pl.when(jnp.logical_and(valid, e == 0))
    def _():
        for s in range(SUB):
            xb_ref[:, s * LANES:(s + 1) * LANES] = xs_ref[pl.ds(s, tile, stride=XSUB), :].astype(BF16)

    @pl.when(jnp.logical_and(jnp.logical_not(valid), e == 0))
    def _():
        ys_ref[...] = jnp.zeros_like(ys_ref)

    @pl.when(valid)
    def _():
        x = xb_ref[...]
        hg = jnp.dot(x, wg_ref[0], preferred_element_type=F32)
        hu = jnp.dot(x, wu_ref[0], preferred_element_type=F32)
        route = xs_ref[pl.ds(SUB, tile, stride=XSUB), :]
        lane = lax.broadcasted_iota(I32, route.shape, 1)
        cw = jnp.sum(jnp.where(lane == e, route, 0.0), axis=-1, keepdims=True)
        hmid = (hg * jax.nn.sigmoid(hg)) * hu * cw
        y = jnp.dot(hmid.astype(BF16), wd_ref[0], preferred_element_type=F32)

        @pl.when(e == 0)
        def _():
            acc_ref[...] = y

        @pl.when(jnp.logical_and(e > 0, e < last_e))
        def _():
            acc_ref[...] += y

        @pl.when(e == last_e)
        def _():
            _store_token_major(ys_ref, acc_ref[...] + y, tile)


def _moe(meta, xs, wg, wu, wd, tile):
    n_rows = xs.shape[0] // XSUB
    nt = n_rows // tile

    def row_map(i, e, meta_ref):
        last = jnp.maximum(meta_ref[META_TOTAL] // tile - 1, 0)
        return (jnp.minimum(i, last), 0)

    def w_map(i, e, meta_ref):
        last = jnp.maximum(meta_ref[META_TOTAL] // tile - 1, 0)
        g = _tile_group(meta_ref, jnp.minimum(i, last), tile)
        return (g * EXPERTS_PER_GROUP + e, 0, 0)

    return pl.pallas_call(
        functools.partial(_moe_body, tile=tile),
        grid_spec=pltpu.PrefetchScalarGridSpec(
            num_scalar_prefetch=1,
            grid=(nt, EXPERTS_PER_GROUP),
            in_specs=[
                pl.BlockSpec((tile * XSUB, LANES), row_map),
                pl.BlockSpec((1, D_MODEL, EXPERT_FF), w_map),
                pl.BlockSpec((1, D_MODEL, EXPERT_FF), w_map),
                pl.BlockSpec((1, EXPERT_FF, D_MODEL), w_map),
            ],
            out_specs=pl.BlockSpec((tile * SUB, LANES), lambda i, e, meta_ref: (i, 0)),
            scratch_shapes=[pltpu.VMEM((tile, D_MODEL), BF16), pltpu.VMEM((tile, D_MODEL), F32)],
        ),
        out_shape=jax.ShapeDtypeStruct((n_rows * SUB, LANES), F32),
        compiler_params=_cparams(("arbitrary", "arbitrary")),
        name="moe",
    )(meta, xs, wg, wu, wd)


def _ln2_body(pos_ref, h_ref, ys_ref, p_ref, wp_ref, wpg_ref, g_ref, b_ref, o_ref, fbuf, sem, *, tm):
    i = pl.program_id(0)
    slot = i % 2

    def copy(step, t, sl):
        return pltpu.make_async_copy(_token_block(ys_ref, pos_ref[step * tm + t]), _token_block(fbuf.at[sl], t),
                                     sem.at[sl])

    def issue(step, sl):
        def body(j, c):
            for k in range(2):
                copy(step, 2 * j + k, sl).start(priority=1)
            return c

        lax.fori_loop(0, tm // 2, body, 0)

    @pl.when(i == 0)
    def _():
        issue(0, 0)

    @pl.when(i + 1 < pl.num_programs(0))
    def _():
        issue(i + 1, 1 - slot)

    h1 = h_ref[...]
    ple = jnp.dot(p_ref[...].astype(BF16), wp_ref[...], preferred_element_type=F32)
    gate = jnp.dot(h1.astype(BF16), wpg_ref[...], preferred_element_type=F32)
    z = DN_ALPHA * h1 + ple * jax.nn.sigmoid(gate)

    def wait(t, c):
        copy(i, t, slot).wait()
        return c

    lax.fori_loop(0, tm, wait, 0)
    ff = jnp.concatenate([fbuf[slot, pl.ds(s, tm, stride=SUB), :] for s in range(SUB)], axis=1)
    o_ref[...] = _layer_norm(z + ff, g_ref[...], b_ref[...])


def _ln2(pos, h1, ys, p, wple, wpg, g, b, tm):
    T = h1.shape[0]
    return pl.pallas_call(
        functools.partial(_ln2_body, tm=tm),
        grid_spec=pltpu.PrefetchScalarGridSpec(
            num_scalar_prefetch=1,
            grid=(T // tm,),
            in_specs=[
                pl.BlockSpec((tm, D_MODEL), lambda i, pos_ref: (i, 0)),
                pl.BlockSpec(memory_space=pl.ANY),
                pl.BlockSpec((tm, PLE_DIM), lambda i, pos_ref: (i, 0)),
                pl.BlockSpec((PLE_DIM, D_MODEL), lambda i, pos_ref: (0, 0)),
                pl.BlockSpec((D_MODEL, D_MODEL), lambda i, pos_ref: (0, 0)),
                pl.BlockSpec((1, D_MODEL), lambda i, pos_ref: (0, 0)),
                pl.BlockSpec((1, D_MODEL), lambda i, pos_ref: (0, 0)),
            ],
            out_specs=pl.BlockSpec((tm, D_MODEL), lambda i, pos_ref: (i, 0)),
            scratch_shapes=[pltpu.VMEM((2, tm * SUB, LANES), F32), pltpu.SemaphoreType.DMA((2,))],
        ),
        out_shape=jax.ShapeDtypeStruct((T, D_MODEL), F32),
        compiler_params=_cparams(("arbitrary",)),
        name="ple_ln2",
    )(pos, h1, ys, p, wple, wpg, g, b)


def _pack_in_weights(w_in_l, m_i_bias_l, m_f_bias_l):
    parts, off = {}, 0
    for name, n in _SPLITS:
        parts[name] = w_in_l[:, off:off + n]
        off += n
    z64 = jnp.zeros((D_MODEL, IDX_DIM), w_in_l.dtype)
    wp = jnp.concatenate(
        [parts["a_q"], parts["m_q"], parts["m_k"], parts["m_v"], parts["m_o"], parts["g_a"], parts["g_m"],
         parts["i_q"], parts["a_ckv"], parts["i_k"], z64, z64, parts["i_k"]], axis=1).astype(BF16)
    pad = jnp.zeros((D_MODEL, LANES - IDX_HEADS - 2 * M_HEADS), w_in_l.dtype)
    ws = jnp.concatenate([parts["i_w"], parts["m_i"], parts["m_f"], pad], axis=1).astype(BF16)
    bias = jnp.concatenate([jnp.zeros((IDX_HEADS,), F32), m_i_bias_l.astype(F32), m_f_bias_l.astype(F32),
                            jnp.zeros((LANES - IDX_HEADS - 2 * M_HEADS,), F32)])
    return wp, ws, ws.T, bias.reshape(1, LANES), bias.reshape(LANES, 1)


def _tiles(T, S):
    return dict(
        inproj_tm=min(1024, T), inproj_tn=1536, row_tm=min(512, T), mix_tn=1024,
        attn_tq=min(256, S), mlstm_l=min(256, S), moe_tile=min(512, T // 4), dma_rows=min(256, T),
    )


def kernel(x, p, w_in, kv_norm, w_ukv, m_i_bias, m_f_bias, w_proj_a, w_proj_m, w_out, ln1_g, ln1_b, w_grp, b_grp,
           w_exp, b_exp, w_gate, w_up, w_down, w_ple, w_ple_gate, ln2_g, ln2_b):
    B, S, D = x.shape
    T = B * S
    assert D == D_MODEL and T % LANES == 0
    cfg = _tiles(T, S)
    h = x.reshape(T, D)
    for l in range(w_in.shape[0]):
        wp, ws, wst, brow, bcol = _pack_in_weights(w_in[l], m_i_bias[l], m_f_bias[l])
        proj, g, gt = _inproj(h, wp, ws, wst, brow, bcol, cfg["inproj_tm"], cfg["inproj_tn"])
        kv = _kv(proj, kv_norm[l].reshape(1, -1).astype(F32), w_ukv[l].astype(BF16), cfg["row_tm"])
        y_a = _attn(proj, kv, g, B, S, cfg["attn_tq"])
        y_m = _mlstm(proj, gt, B, S, cfg["mlstm_l"])
        mix = _mix(y_a, y_m, w_proj_a[l].astype(BF16), w_proj_m[l].astype(BF16), proj, cfg["row_tm"], cfg["mix_tn"])

        n_route = N_GROUPS + N_EXPERTS
        wr = jnp.concatenate([w_grp[l], w_exp[l], jnp.zeros((D, LANES - n_route), F32)], axis=1)
        br = jnp.concatenate([b_grp[l], b_exp[l], jnp.zeros((LANES - n_route,), F32)])
        h1, route, rt = _ln1(mix, h, w_out[l].astype(BF16), ln1_g[l].reshape(1, D), ln1_b[l].reshape(1, D),
                             wr.T.astype(BF16), br.reshape(LANES, 1), cfg["row_tm"])

        tile = cfg["moe_tile"]
        pos, meta = _positions(rt[N_GROUPS].reshape(T // LANES, LANES), tile)
        pos = pos.reshape(T)
        meta = meta.reshape(LANES)
        xs = _scatter_rows(pos, meta, h1, route, T + N_GROUPS * tile, cfg["dma_rows"])
        ys = _moe(meta, xs, w_gate[l].astype(BF16), w_up[l].astype(BF16), w_down[l].astype(BF16), tile)
        h = _ln2(pos, h1, ys, p[l].reshape(T, PLE_DIM), w_ple[l].astype(BF16), w_ple_gate[l].astype(BF16),
                 ln2_g[l].reshape(1, D), ln2_b[l].reshape(1, D), cfg["row_tm"])
    return h.reshape(B, S, D)
```
